```python
import math
import jax
import jax.numpy as jnp
from jax import lax
import numpy as np

D_MODEL = 1024
BATCH = 4
SEQ = 4096
DEPTH = 2
DEC_BATCH = 16
DEC_SEQ = 2048
PAST_LEN = 128

N_EVEN = (DEPTH + 1) // 2
N_ODD = DEPTH // 2
NORM_EPS = 1e-6
ROPE_THETA = 10000.0
CHUNK = 128

SSD_HEADS = 16
SSD_HEAD_DIM = 64
SSD_WIDTH = SSD_HEADS * SSD_HEAD_DIM
SSD_GROUPS = 2
SSD_HPG = SSD_HEADS // SSD_GROUPS
SSD_STATE = 128
SSD_CONV = 5
SSD_CONV_CH = SSD_WIDTH + 2 * SSD_GROUPS * SSD_STATE

RET_HEADS = 8
RET_QK_DIM = 64
RET_V_DIM = 128
RET_QK_WIDTH = RET_HEADS * RET_QK_DIM
RET_V_WIDTH = RET_HEADS * RET_V_DIM

EV_SPLITS = (SSD_WIDTH, SSD_CONV_CH, 2 * SSD_HEADS, RET_QK_WIDTH, RET_QK_WIDTH, RET_V_WIDTH, RET_V_WIDTH)
EV_PROJ = sum(EV_SPLITS)
EV_MIX = SSD_WIDTH + RET_V_WIDTH

ATT_HEADS = 16
ATT_HEAD_DIM = 64
ATT_WIDTH = ATT_HEADS * ATT_HEAD_DIM
DILATED_PATTERNS = ((128, 1), (512, 4), (2048, 16))

GMLP_GROUPS = 8
GMLP_GROUP_DIM = 64
GMLP_WIDTH = GMLP_GROUPS * GMLP_GROUP_DIM
GMLP_CHUNK = 128

OD_SPLITS = (ATT_WIDTH, ATT_WIDTH, ATT_WIDTH, GMLP_WIDTH, GMLP_WIDTH)
OD_PROJ = sum(OD_SPLITS)
OD_MIX = ATT_WIDTH + GMLP_WIDTH

FFN_HIDDEN = 4 * D_MODEL

kernel_name = "hybrid_bidir_ssd_retention_dilated_gmlp_encoder"


def split_cols(t, sizes):
    cuts = [int(c) for c in np.cumsum(sizes)[:-1]]
    return jnp.split(t, cuts, axis=-1)


def flip_seq(t):
    return jnp.flip(t, axis=1)


def rms_norm(x, w):
    xf = x.astype(jnp.float32)
    y = xf * lax.rsqrt(jnp.mean(xf * xf, axis=-1, keepdims=True) + NORM_EPS)
    return (y * w.astype(jnp.float32)).astype(x.dtype)


def rotary(t):
    S, D = t.shape[1], t.shape[-1]
    inv_freq = ROPE_THETA ** (-jnp.arange(0, D, 2, dtype=jnp.float32) / D)
    ang = jnp.arange(S, dtype=jnp.float32)[:, None] * inv_freq[None, :]
    ang = jnp.concatenate([ang, ang], axis=-1)[None, :, None, :]
    tf = t.astype(jnp.float32)
    t1, t2 = jnp.split(tf, 2, axis=-1)
    rot = jnp.concatenate([-t2, t1], axis=-1)
    return (tf * jnp.cos(ang) + rot * jnp.sin(ang)).astype(t.dtype)


def carry_states(states, decay):
    def step(h, inp):
        s_c, d_c = inp
        return d_c * h + s_c, h
    h0 = jnp.zeros_like(states[:, 0])
    _, h_in = lax.scan(step, h0, (jnp.moveaxis(states, 1, 0), jnp.moveaxis(decay, 1, 0)))
    return jnp.moveaxis(h_in, 0, 1)


def ssd_chunked(x, log_a, bm, cm):
    Bsz, L, G, E, P = x.shape
    N = bm.shape[-1]
    nc = L // CHUNK
    xc = x.reshape(Bsz, nc, CHUNK, G, E, P)
    bc = bm.reshape(Bsz, nc, CHUNK, G, N)
    cc = cm.reshape(Bsz, nc, CHUNK, G, N)
    a_cs = jnp.cumsum(log_a.astype(jnp.float32).reshape(Bsz, nc, CHUNK, G, E), axis=2)
    a_cs = jnp.transpose(a_cs, (0, 1, 3, 4, 2))
    lower = jnp.tril(jnp.ones((CHUNK, CHUNK), dtype=bool))
    seg = a_cs[..., :, None] - a_cs[..., None, :]
    decay = jnp.exp(jnp.where(lower, seg, -jnp.inf)).astype(x.dtype)
    cb = jnp.einsum('bclgn,bcsgn->bcgls', cc, bc)
    y_diag = jnp.einsum('bcgels,bcsgep->bclgep', cb[:, :, :, None] * decay, xc)
    to_end = jnp.exp(a_cs[..., -1:] - a_cs).astype(x.dtype)
    states = jnp.einsum('bcsgn,bcges,bcsgep->bcgepn', bc, to_end, xc)
    chunk_decay = jnp.exp(a_cs[..., -1]).astype(x.dtype)[..., None, None]
    h_in = carry_states(states, chunk_decay)
    y_off = jnp.einsum('bclgn,bcgepn,bcgel->bclgep', cc, h_in, jnp.exp(a_cs).astype(x.dtype))
    return (y_diag + y_off).reshape(Bsz, L, G, E, P)


def retention_chunked(q, k, v, log_gamma):
    Bsz, L, H, DK = q.shape
    DV = v.shape[-1]
    nc = L // CHUNK
    qc = q.reshape(Bsz, nc, CHUNK, H, DK)
    kc = k.reshape(Bsz, nc, CHUNK, H, DK)
    vc = v.reshape(Bsz, nc, CHUNK, H, DV)
    pos = jnp.arange(CHUNK, dtype=jnp.float32)
    dist = pos[:, None] - pos[None, :]
    lg = log_gamma[:, None, None]
    dmat = jnp.exp(jnp.where(dist >= 0, lg * dist, -jnp.inf)).astype(q.dtype)
    scores = jnp.einsum('bclhd,bcshd->bchls', qc, kc) * dmat
    y_in = jnp.einsum('bchls,bcshv->bclhv', scores, vc)
    key_decay = jnp.exp(log_gamma[:, None] * (CHUNK - 1.0 - pos)).astype(q.dtype)
    states = jnp.einsum('bcshd,hs,bcshv->bchdv', kc, key_decay, vc)
    chunk_decay = jnp.broadcast_to(jnp.exp(log_gamma * CHUNK).astype(q.dtype)[None, None, :, None, None], (1, nc, H, 1, 1))
    r_in = carry_states(states, chunk_decay)
    q_decay = jnp.exp(log_gamma[:, None] * (pos + 1.0)).astype(q.dtype)
    y_cross = jnp.einsum('bclhd,bchdv,hl->bclhv', qc, r_in, q_decay)
    return (y_in + y_cross).reshape(Bsz, L, H, DV)


def depthwise_conv(x, w, b):
    C = x.shape[-1]
    K = w.shape[0]
    y = lax.conv_general_dilated(x, w[:, None, :].astype(x.dtype), window_strides=(1,), padding=[(K // 2, K // 2)],
                                 dimension_numbers=('NWC', 'WIO', 'NWC'), feature_group_count=C)
    return y + b.astype(x.dtype)


def gated_group_rms(y, z, w):
    g = (y * jax.nn.silu(z)).astype(jnp.float32)
    gs = g.reshape(*g.shape[:-1], SSD_GROUPS, SSD_WIDTH // SSD_GROUPS)
    gs = gs * lax.rsqrt(jnp.mean(gs * gs, axis=-1, keepdims=True) + NORM_EPS)
    return (gs.reshape(g.shape) * w.astype(jnp.float32)).astype(y.dtype)


def head_group_norm(y, w):
    Bsz, S, H, Dv = y.shape
    yf = y.astype(jnp.float32)
    mu = jnp.mean(yf, axis=-1, keepdims=True)
    var = jnp.mean(jnp.square(yf - mu), axis=-1, keepdims=True)
    yn = ((yf - mu) * lax.rsqrt(var + NORM_EPS)).reshape(Bsz, S, H * Dv)
    return (yn * w.astype(jnp.float32)).astype(y.dtype)


def dilated_window_attention(q, k, v, half, dilation):
    Bsz, S, H, D = q.shape
    r = dilation
    L = S // r
    blk = half
    nb = -(-L // blk)
    Lp = nb * blk

    def by_stride(t):
        return jnp.transpose(t.reshape(Bsz, L, r, H, D), (0, 2, 1, 3, 4))

    qs, ks, vs = by_stride(q), by_stride(k), by_stride(v)
    qb = jnp.pad(qs, ((0, 0), (0, 0), (0, Lp - L), (0, 0), (0, 0))).reshape(Bsz, r, nb, blk, H, D)
    pad_kv = ((0, 0), (0, 0), (blk, Lp - L + blk), (0, 0), (0, 0))

    def neighbour_blocks(t):
        tp = jnp.pad(t, pad_kv).reshape(Bsz, r, nb + 2, blk, H, D)
        return jnp.concatenate([tp[:, :, :-2], tp[:, :, 1:-1], tp[:, :, 2:]], axis=3)

    kb, vb = neighbour_blocks(ks), neighbour_blocks(vs)
    qpos = jnp.arange(nb)[:, None] * blk + jnp.arange(blk)[None, :]
    kpos = jnp.arange(nb)[:, None] * blk - blk + jnp.arange(3 * blk)[None, :]
    qp, kp = qpos[:, :, None], kpos[:, None, :]
    valid = ((jnp.abs(kp - qp) <= half) & (kp >= 0) & (kp < L)) | (kp == qp)
    s = jnp.einsum('brnqhd,brnkhd->brnhqk', qb, kb).astype(jnp.float32) * (D ** -0.5)
    s = jnp.where(valid[None, None, :, None], s, -jnp.inf)
    m = jnp.max(s, axis=-1, keepdims=True)
    p = jnp.exp(s - m)
    den = jnp.sum(p, axis=-1)
    o = jnp.einsum('brnhqk,brnkhd->brnqhd', p.astype(v.dtype), vb).astype(jnp.float32)
    o = o / jnp.transpose(den, (0, 1, 2, 4, 3))[..., None]
    lse = jnp.transpose(m[..., 0] + jnp.log(den), (0, 1, 2, 4, 3))
    o = o.reshape(Bsz, r, Lp, H, D)[:, :, :L]
    lse = lse.reshape(Bsz, r, Lp, H)[:, :, :L]
    o = jnp.transpose(o, (0, 2, 1, 3, 4)).reshape(Bsz, S, H, D)
    lse = jnp.transpose(lse, (0, 2, 1, 3)).reshape(Bsz, S, H)
    return o, lse


def chunk_spatial_gate(u, vg, norm_w, w_s, b_s):
    Bsz, S, _ = u.shape
    nc = S // GMLP_CHUNK
    vf = vg.astype(jnp.float32)
    mu = jnp.mean(vf, axis=-1, keepdims=True)
    var = jnp.mean(jnp.square(vf - mu), axis=-1, keepdims=True)
    vn = ((vf - mu) * lax.rsqrt(var + NORM_EPS) * norm_w.astype(jnp.float32)).astype(vg.dtype)
    vc = vn.reshape(Bsz, nc, GMLP_CHUNK, GMLP_GROUPS, GMLP_GROUP_DIM)
    mixed = jnp.einsum('gts,bcsgd->bctgd', w_s.astype(vc.dtype), vc) + jnp.transpose(b_s).astype(vc.dtype)[:, :, None]
    return u * mixed.reshape(Bsz, S, GMLP_WIDTH)


def even_mixer(h, in_proj, conv_w, conv_b, dt_bias, a_log, d_skip, ssd_norm_w, ret_decay, ret_gn_w, out_proj):
    Bsz, S, _ = h.shape
    z, xbc, dt_raw, q, k, v, g = split_cols(h @ in_proj, EV_SPLITS)
    xbc = jax.nn.silu(depthwise_conv(xbc, conv_w, conv_b))
    xs, bm, cm = split_cols(xbc, (SSD_WIDTH, SSD_GROUPS * SSD_STATE, SSD_GROUPS * SSD_STATE))
    xs = xs.reshape(Bsz, S, SSD_GROUPS, SSD_HPG, SSD_HEAD_DIM)
    bm = bm.reshape(Bsz, S, SSD_GROUPS, SSD_STATE)
    cm = cm.reshape(Bsz, S, SSD_GROUPS, SSD_STATE)
    dt = jax.nn.softplus(dt_raw.astype(jnp.float32).reshape(Bsz, S, 2, SSD_HEADS) + dt_bias.astype(jnp.float32))
    a = -jnp.exp(a_log.astype(jnp.float32))
    log_a = (dt * a).reshape(Bsz, S, 2, SSD_GROUPS, SSD_HPG)
    dt = dt.reshape(Bsz, S, 2, SSD_GROUPS, SSD_HPG).astype(xs.dtype)
    y_f = ssd_chunked(xs * dt[:, :, 0, ..., None], log_a[:, :, 0], bm, cm)
    y_b = flip_seq(ssd_chunked(flip_seq(xs * dt[:, :, 1, ..., None]), flip_seq(log_a[:, :, 1]), flip_seq(bm), flip_seq(cm)))
    y = y_f + y_b + xs * d_skip.reshape(SSD_GROUPS, SSD_HPG, 1).astype(xs.dtype)
    y_ssd = gated_group_rms(y.reshape(Bsz, S, SSD_WIDTH), z, ssd_norm_w)
    q = rotary(q.reshape(Bsz, S, RET_HEADS, RET_QK_DIM))
    k = rotary(k.reshape(Bsz, S, RET_HEADS, RET_QK_DIM)) * (RET_QK_DIM ** -0.5)
    v = v.reshape(Bsz, S, RET_HEADS, RET_V_DIM)
    lg = jax.nn.log_sigmoid(ret_decay.astype(jnp.float32))
    r = retention_chunked(q, k, v, lg[0]) + flip_seq(retention_chunked(flip_seq(q), flip_seq(k), flip_seq(v), lg[1]))
    y_ret = head_group_norm(r, ret_gn_w) * jax.nn.silu(g)
    return jnp.concatenate([y_ssd, y_ret], axis=-1) @ out_proj


def odd_mixer(h, in_proj, gmlp_norm_w, gmlp_ws, gmlp_bs, out_proj):
    Bsz, S, _ = h.shape
    q, k, v, u, vg = split_cols(h @ in_proj, OD_SPLITS)
    q = rotary(q.reshape(Bsz, S, ATT_HEADS, ATT_HEAD_DIM))
    k = rotary(k.reshape(Bsz, S, ATT_HEADS, ATT_HEAD_DIM))
    v = v.reshape(Bsz, S, ATT_HEADS, ATT_HEAD_DIM)
    outs, lses = [], []
    for window, dilation in DILATED_PATTERNS:
        o, lse = dilated_window_attention(q, k, v, window // (2 * dilation), dilation)
        outs.append(o)
        lses.append(lse)
    wts = jax.nn.softmax(jnp.stack(lses, axis=0), axis=0)
    att = jnp.einsum('pbsh,pbshd->bshd', wts, jnp.stack(outs, axis=0)).astype(h.dtype).reshape(Bsz, S, ATT_WIDTH)
    sg = chunk_spatial_gate(u, vg, gmlp_norm_w, gmlp_ws, gmlp_bs)
    return jnp.concatenate([att, sg], axis=-1) @ out_proj


def trunk(x, norm_mix_pre, norm_mix_post, norm_ffn_pre, norm_ffn_post, ffn_w1, ffn_w2,
          ev_in_proj, ev_conv_w, ev_conv_b, ssd_dt_bias, ssd_a_log, ssd_d, ssd_norm_w, ret_decay, ret_gn_w, ev_out_proj,
          od_in_proj, gmlp_norm_w, gmlp_ws, gmlp_bs, od_out_proj):
    for i in range(DEPTH):
        h = rms_norm(x, norm_mix_pre[i])
        j = i // 2
        if i % 2 == 0:
            mix = even_mixer(h, ev_in_proj[j], ev_conv_w[j], ev_conv_b[j], ssd_dt_bias[j], ssd_a_log[j], ssd_d[j],
                             ssd_norm_w[j], ret_decay[j], ret_gn_w[j], ev_out_proj[j])
        else:
            mix = odd_mixer(h, od_in_proj[j], gmlp_norm_w[j], gmlp_ws[j], gmlp_bs[j], od_out_proj[j])
        x = x + rms_norm(mix, norm_mix_post[i])
        h = rms_norm(x, norm_ffn_pre[i])
        f = jnp.square(jax.nn.relu(h @ ffn_w1[i])) @ ffn_w2[i]
        x = x + rms_norm(f, norm_ffn_post[i])
    return x


def setup_inputs(seed: int = 0) -> dict:
    key = jax.random.key(seed)
    ks = jax.random.split(key, 23)
    f32 = jnp.float32

    def normal(k, shape, scale):
        return jax.random.normal(k, shape, f32) * scale

    def gain(k, shape):
        return 1.0 + 0.05 * jax.random.normal(k, shape, f32)

    dt0 = jnp.exp(jax.random.uniform(ks[11], (N_EVEN, 2, SSD_HEADS), f32, math.log(1e-3), math.log(1e-1)))
    ssd_dt_bias = dt0 + jnp.log(-jnp.expm1(-dt0))
    ssd_a_log = jnp.log(jax.random.uniform(ks[12], (N_EVEN, 2, SSD_HEADS), f32, 1.0, 16.0))
    ret_base = jnp.log(jnp.exp2(5.0 + jnp.arange(RET_HEADS, dtype=f32)) - 1.0)
    ret_decay = ret_base[None, None, :] + normal(ks[15], (N_EVEN, 2, RET_HEADS), 0.1)
    return {
        "x_prompt": normal(ks[0], (BATCH, SEQ, D_MODEL), 1.0),
        "x_sample": normal(ks[1], (DEC_BATCH, DEC_SEQ, D_MODEL), 1.0),
        "norm_mix_pre": gain(ks[2], (DEPTH, D_MODEL)),
        "norm_mix_post": gain(ks[3], (DEPTH, D_MODEL)),
        "norm_ffn_pre": gain(ks[4], (DEPTH, D_MODEL)),
        "norm_ffn_post": gain(ks[5], (DEPTH, D_MODEL)),
        "ffn_w1": normal(ks[6], (DEPTH, D_MODEL, FFN_HIDDEN), D_MODEL ** -0.5),
        "ffn_w2": normal(ks[7], (DEPTH, FFN_HIDDEN, D_MODEL), FFN_HIDDEN ** -0.5),
        "ev_in_proj": normal(ks[8], (N_EVEN, D_MODEL, EV_PROJ), D_MODEL ** -0.5),
        "ev_conv_w": normal(ks[9], (N_EVEN, SSD_CONV, SSD_CONV_CH), SSD_CONV ** -0.5),
        "ev_conv_b": normal(ks[10], (N_EVEN, SSD_CONV_CH), 0.02),
        "ssd_dt_bias": ssd_dt_bias,
        "ssd_a_log": ssd_a_log,
        "ssd_d": gain(ks[13], (N_EVEN, SSD_HEADS)),
        "ssd_norm_w": gain(ks[14], (N_EVEN, SSD_WIDTH)),
        "ret_decay": ret_decay,
        "ret_gn_w": gain(ks[16], (N_EVEN, RET_V_WIDTH)),
        "ev_out_proj": normal(ks[17], (N_EVEN, EV_MIX, D_MODEL), EV_MIX ** -0.5),
        "od_in_proj": normal(ks[18], (N_ODD, D_MODEL, OD_PROJ), D_MODEL ** -0.5),
        "gmlp_norm_w": gain(ks[19], (N_ODD, GMLP_WIDTH)),
        "gmlp_ws": normal(ks[20], (N_ODD, GMLP_GROUPS, GMLP_CHUNK, GMLP_CHUNK), GMLP_CHUNK ** -0.5),
        "gmlp_bs": 1.0 + normal(ks[21], (N_ODD, GMLP_GROUPS, GMLP_CHUNK), 0.1),
        "od_out_proj": normal(ks[22], (N_ODD, OD_MIX, D_MODEL), OD_MIX ** -0.5),
    }


def reference(x_prompt, x_sample, norm_mix_pre, norm_mix_post, norm_ffn_pre, norm_ffn_post, ffn_w1, ffn_w2,
              ev_in_proj, ev_conv_w, ev_conv_b, ssd_dt_bias, ssd_a_log, ssd_d, ssd_norm_w, ret_decay, ret_gn_w,
              ev_out_proj, od_in_proj, gmlp_norm_w, gmlp_ws, gmlp_bs, od_out_proj):
    weights = (norm_mix_pre, norm_mix_post, norm_ffn_pre, norm_ffn_post, ffn_w1, ffn_w2,
               ev_in_proj, ev_conv_w, ev_conv_b, ssd_dt_bias, ssd_a_log, ssd_d, ssd_norm_w, ret_decay, ret_gn_w,
               ev_out_proj, od_in_proj, gmlp_norm_w, gmlp_ws, gmlp_bs, od_out_proj)
    y_prompt = trunk(x_prompt, *weights)
    y_sample = trunk(x_sample, *weights)
    return (y_prompt, y_sample)
```

```python
import functools
import math

import jax
import jax.numpy as jnp
from jax import lax
from jax.experimental import pallas as pl
from jax.experimental.pallas import tpu as pltpu

F32 = jnp.float32
BF16 = jnp.bfloat16

D_MODEL = 1024
NORM_EPS = 1e-6
ROPE_THETA = 10000.0
CHUNK = 128
LANES = 128

SSD_HEADS = 16
SSD_HEAD_DIM = 64
SSD_WIDTH = SSD_HEADS * SSD_HEAD_DIM
SSD_GROUPS = 2
SSD_HPG = SSD_HEADS // SSD_GROUPS
SSD_STATE = 128
SSD_CONV = 5
SSD_BC = 2 * SSD_GROUPS * SSD_STATE
SSD_CONV_CH = SSD_WIDTH + SSD_BC

RET_HEADS = 8
RET_QK_DIM = 64
RET_V_DIM = 128
RET_QK_WIDTH = RET_HEADS * RET_QK_DIM
RET_V_WIDTH = RET_HEADS * RET_V_DIM
EV_MIX = SSD_WIDTH + RET_V_WIDTH

ATT_HEADS = 16
ATT_HEAD_DIM = 64
ATT_WIDTH = ATT_HEADS * ATT_HEAD_DIM
DILATED_PATTERNS = ((128, 1), (512, 4), (2048, 16))
ATT_HALF = 64

GMLP_GROUPS = 8
GMLP_GROUP_DIM = 64
GMLP_WIDTH = GMLP_GROUPS * GMLP_GROUP_DIM
OD_MIX = ATT_WIDTH + GMLP_WIDTH
FFN_HIDDEN = 4 * D_MODEL

EV_Z0, EV_XBC0, EV_Q0, EV_K0, EV_V0, EV_G0, EV_END = 0, 1024, 2560, 3072, 3584, 4608, 5632
DT_LANE, CS_LANE, TOT_LANE = 0, 16, 32

ROW_TILE = 256
FFN_ROW_TILE = 512
VMEM_LIMIT = 56 * 1024 * 1024
NEG_BIG = -1e30


def _params(*sem):
    return pltpu.CompilerParams(dimension_semantics=sem, vmem_limit_bytes=VMEM_LIMIT)


def _rms(x, w):
    return x * lax.rsqrt(jnp.mean(x * x, axis=-1, keepdims=True) + NORM_EPS) * w


def _silu(x):
    return x / (1.0 + jnp.exp(-x))


def _softplus(x):
    return jnp.maximum(x, 0.0) + jnp.log1p(jnp.exp(-jnp.abs(x)))


def _rope(t, cos, sin_signed, first_half):
    rot = jnp.where(first_half, pltpu.roll(t, 96, 1), pltpu.roll(t, 32, 1))
    return t * cos + rot * sin_signed


def _rope_cols(a, cos, sin_signed):
    lane = lax.broadcasted_iota(jnp.int32, (a.shape[0], LANES), 1)
    first_half = (lane % 64) < 32
    return [_rope(a[:, j * LANES:(j + 1) * LANES], cos, sin_signed, first_half) for j in range(a.shape[1] // LANES)]


def _split3(x):
    hi = x.astype(BF16)
    r1 = x - hi.astype(F32)
    mid = r1.astype(BF16)
    lo = (r1 - mid.astype(F32)).astype(BF16)
    return hi, mid, lo


def _ev_inproj_kernel(x_ref, nw_ref, w_ref, wdt_ref, dtb_ref, alog_ref, cos_ref, sin_ref,
                      z_ref, xbc_ref, q_ref, k_ref, v_ref, g_ref, dt_ref, dtt_ref):
    tm = x_ref.shape[0]
    hn = _rms(x_ref[...], nw_ref[...]).astype(BF16)

    def mm(c0, c1):
        return jnp.dot(hn, w_ref[:, c0:c1], preferred_element_type=F32)

    z_ref[...] = mm(EV_Z0, EV_XBC0).astype(BF16)
    xbc_ref[...] = mm(EV_XBC0, EV_Q0).astype(BF16)
    cos, sin = cos_ref[...], sin_ref[...]
    for j, t in enumerate(_rope_cols(mm(EV_Q0, EV_K0), cos, sin)):
        q_ref[:, j * LANES:(j + 1) * LANES] = t.astype(BF16)
    for j, t in enumerate(_rope_cols(mm(EV_K0, EV_V0), cos, sin)):
        k_ref[:, j * LANES:(j + 1) * LANES] = (t * (RET_QK_DIM ** -0.5)).astype(BF16)
    v_ref[...] = mm(EV_V0, EV_G0).astype(BF16)
    g_ref[...] = mm(EV_G0, EV_END).astype(BF16)

    raw = jnp.dot(hn, wdt_ref[...], preferred_element_type=F32)
    dt = _softplus(raw + dtb_ref[...])
    la = dt * (-jnp.exp(alog_ref[...]))
    lane = lax.broadcasted_iota(jnp.int32, (CHUNK, LANES), 1)
    sub = lane % 64
    is_dt = sub < CS_LANE
    is_cs = (sub >= CS_LANE) & (sub < TOT_LANE)
    is_bwd = lane >= 64
    row = lax.broadcasted_iota(jnp.int32, (CHUNK, CHUNK), 0)
    col = lax.broadcasted_iota(jnp.int32, (CHUNK, CHUNK), 1)
    tri = jnp.where(row >= col, 1.0, 0.0).astype(BF16)
    for c in range(tm // CHUNK):
        rows = slice(c * CHUNK, (c + 1) * CHUNK)
        la_c = la[rows]
        cs = sum(jnp.dot(tri, part, preferred_element_type=F32) for part in _split3(la_c))
        tot = jnp.broadcast_to(cs[CHUNK - 1:CHUNK, :], cs.shape)
        rcs = tot - cs + la_c
        rec = jnp.where(is_dt, dt[rows], jnp.where(is_cs, jnp.where(is_bwd, rcs, cs), tot))
        rec_b = pltpu.roll(rec, 64, 1)
        dt_ref[0, rows, :] = rec
        dt_ref[1, rows, :] = rec_b
        dtt_ref[0, rows, :] = rec.T
        dtt_ref[1, rows, :] = rec_b.T


def _ev_inproj(x2d, seq, nw, w_main, w_dt, dt_bias, a_log, cos, sin):
    t = x2d.shape[0]
    tm = ROW_TILE
    nseq = seq // tm
    row = lambda i: (i, 0)
    const = lambda i: (0, 0)
    bf = lambda n: jax.ShapeDtypeStruct((t, n), BF16)
    return pl.pallas_call(
        _ev_inproj_kernel,
        grid=(t // tm,),
        in_specs=[
            pl.BlockSpec((tm, D_MODEL), row),
            pl.BlockSpec((1, D_MODEL), const),
            pl.BlockSpec(w_main.shape, const),
            pl.BlockSpec(w_dt.shape, const),
            pl.BlockSpec((1, LANES), const),
            pl.BlockSpec((1, LANES), const),
            pl.BlockSpec((tm, LANES), lambda i: (i % nseq, 0)),
            pl.BlockSpec((tm, LANES), lambda i: (i % nseq, 0)),
        ],
        out_specs=[
            pl.BlockSpec((tm, SSD_WIDTH), row),
            pl.BlockSpec((tm, SSD_CONV_CH), row),
            pl.BlockSpec((tm, RET_QK_WIDTH), row),
            pl.BlockSpec((tm, RET_QK_WIDTH), row),
            pl.BlockSpec((tm, RET_V_WIDTH), row),
            pl.BlockSpec((tm, RET_V_WIDTH), row),
            pl.BlockSpec((2, tm, LANES), lambda i: (0, i, 0)),
            pl.BlockSpec((2, tm, LANES), lambda i: (0, i, 0)),
        ],
        out_shape=[bf(SSD_WIDTH), bf(SSD_CONV_CH), bf(RET_QK_WIDTH), bf(RET_QK_WIDTH), bf(RET_V_WIDTH), bf(RET_V_WIDTH),
                   jax.ShapeDtypeStruct((2, t, LANES), F32), jax.ShapeDtypeStruct((2, t, LANES), F32)],
        compiler_params=_params("parallel"),
        name="ev_inproj",
    )(x2d, nw, w_main, w_dt, dt_bias, a_log, cos, sin)


CONV_HALO = 16


def _conv_kernel(main_ref, prev_ref, next_ref, w_ref, b_ref, o_ref):
    i = pl.program_id(1)
    last = pl.num_programs(1) - 1
    tc = main_ref.shape[0]
    prev = jnp.where(i > 0, prev_ref[...].astype(F32), 0.0)
    nxt = jnp.where(i < last, next_ref[...].astype(F32), 0.0)
    ext = jnp.concatenate([prev, main_ref[...].astype(F32), nxt], axis=0)
    acc = jnp.broadcast_to(b_ref[...], (tc, SSD_CONV_CH))
    for j in range(SSD_CONV):
        off = CONV_HALO - SSD_CONV // 2 + j
        acc = acc + w_ref[j:j + 1, :] * ext[off:off + tc, :]
    o_ref[...] = _silu(acc).astype(BF16)


def _conv(xbc3d, conv_w, conv_b):
    b, s, c = xbc3d.shape
    tc = ROW_TILE
    per = tc // CONV_HALO
    nh = s // CONV_HALO
    return pl.pallas_call(
        _conv_kernel,
        grid=(b, s // tc),
        in_specs=[
            pl.BlockSpec((None, tc, c), lambda bi, i: (bi, i, 0)),
            pl.BlockSpec((None, CONV_HALO, c), lambda bi, i: (bi, jnp.maximum(i * per - 1, 0), 0)),
            pl.BlockSpec((None, CONV_HALO, c), lambda bi, i: (bi, jnp.minimum((i + 1) * per, nh - 1), 0)),
            pl.BlockSpec((8, c), lambda bi, i: (0, 0)),
            pl.BlockSpec((1, c), lambda bi, i: (0, 0)),
        ],
        out_specs=pl.BlockSpec((None, tc, c), lambda bi, i: (bi, i, 0)),
        out_shape=jax.ShapeDtypeStruct((b, s, c), BF16),
        compiler_params=_params("parallel", "parallel"),
        name="ev_conv",
    )(xbc3d, xbc3d, xbc3d, conv_w, conv_b)


def _ssd_ret_kernel(xs_ref, bc_ref, dt_ref, dtt_ref, q_ref, k_ref, v_ref, rd_ref, y_ref, h_ref, r_ref):
    d = pl.program_id(1)
    step = pl.program_id(2)

    @pl.when(step == 0)
    def _():
        h_ref[...] = jnp.zeros_like(h_ref)
        r_ref[...] = jnp.zeros_like(r_ref)

    fwd = d == 0
    row = lax.broadcasted_iota(jnp.int32, (CHUNK, CHUNK), 0)
    col = lax.broadcasted_iota(jnp.int32, (CHUNK, CHUNK), 1)
    ahead = jnp.where(fwd, row - col, col - row)
    causal = ahead >= 0

    rec = dt_ref[...]
    rect = dtt_ref[...]
    cscol = rec[:, CS_LANE:CS_LANE + SSD_HEADS]
    ecs = jnp.exp(cscol)
    for g in range(SSD_GROUPS):
        bg = bc_ref[:, g * SSD_STATE:(g + 1) * SSD_STATE]
        cg = bc_ref[:, (SSD_GROUPS + g) * SSD_STATE:(SSD_GROUPS + g + 1) * SSD_STATE]
        cb = lax.dot_general(cg, bg, (((1,), (1,)), ((), ())), preferred_element_type=F32)
        hg = h_ref[g]
        yoff = jnp.dot(cg, hg.astype(BF16), preferred_element_type=F32)
        bgt = bg.astype(F32).T
        s_parts = []
        dec_parts = []
        for he in range(SSD_HPG):
            e = g * SSD_HPG + he
            lanes = slice(e * SSD_HEAD_DIM, (e + 1) * SSD_HEAD_DIM)
            xe = xs_ref[:, lanes].astype(F32)
            seg = cscol[:, e:e + 1] - rect[CS_LANE + e:CS_LANE + e + 1, :]
            decay = jnp.where(causal, jnp.exp(seg), 0.0)
            me = (cb * decay * rect[DT_LANE + e:DT_LANE + e + 1, :]).astype(BF16)
            yd = jnp.dot(me, xe.astype(BF16), preferred_element_type=F32)
            yo = yoff[:, he * SSD_HEAD_DIM:(he + 1) * SSD_HEAD_DIM] * ecs[:, e:e + 1]
            y_ref[:, lanes] = (yd + yo).astype(BF16)
            bw = (bgt * rect[DT_LANE + e:DT_LANE + e + 1, :]
                  * jnp.exp(rect[TOT_LANE + e:TOT_LANE + e + 1, :] - rect[CS_LANE + e:CS_LANE + e + 1, :])).astype(BF16)
            s_parts.append(jnp.dot(bw, xe.astype(BF16), preferred_element_type=F32))
            dec_parts.append(jnp.exp(rect[TOT_LANE + e:TOT_LANE + e + 1, 0:SSD_HEAD_DIM]))
        for he in range(SSD_HPG):
            lanes = slice(he * SSD_HEAD_DIM, (he + 1) * SSD_HEAD_DIM)
            h_ref[g, :, lanes] = hg[:, lanes] * dec_parts[he] + s_parts[he]

    rd = rd_ref[...]
    lg = jnp.minimum(rd, 0.0) - jnp.log1p(jnp.exp(-jnp.abs(rd)))
    rowf = row.astype(F32)
    dist = ahead.astype(F32)
    posq = jnp.where(fwd, rowf + 1.0, CHUNK - rowf)
    posk = jnp.where(fwd, CHUNK - 1.0 - rowf, rowf)
    for h in range(RET_HEADS):
        lgh = lg[h:h + 1, :]
        qh = q_ref[:, h * RET_QK_DIM:(h + 1) * RET_QK_DIM]
        kh = k_ref[:, h * RET_QK_DIM:(h + 1) * RET_QK_DIM]
        vh = v_ref[:, h * RET_V_DIM:(h + 1) * RET_V_DIM]
        dmat = jnp.where(dist >= 0.0, jnp.exp(lgh * dist), 0.0)
        sc = lax.dot_general(qh, kh, (((1,), (1,)), ((), ())), preferred_element_type=F32) * dmat
        qd = (qh.astype(F32) * jnp.exp(lgh * posq)[:, 0:RET_QK_DIM]).astype(BF16)
        rh = r_ref[h]
        y = (jnp.dot(sc.astype(BF16), vh, preferred_element_type=F32)
             + jnp.dot(qd, rh.astype(BF16), preferred_element_type=F32))
        y_ref[:, SSD_WIDTH + h * RET_V_DIM:SSD_WIDTH + (h + 1) * RET_V_DIM] = y.astype(BF16)
        kd = kh.astype(F32) * jnp.exp(lgh * posk)[:, 0:RET_QK_DIM]
        st = jnp.dot(kd.T.astype(BF16), vh, preferred_element_type=F32)
        r_ref[h] = rh * jnp.exp(lgh * float(CHUNK)) + st


def _ssd_ret(xbc_act, dt, dtt, q, k, v, rd):
    b, s, _ = xbc_act.shape
    nc = s // CHUNK

    def chunk(d, i):
        return i + d * (nc - 1 - 2 * i)

    seq = lambda bi, d, i: (bi, chunk(d, i), 0)
    rec = lambda bi, d, i: (d, bi, chunk(d, i), 0)
    return pl.pallas_call(
        _ssd_ret_kernel,
        grid=(b, 2, nc),
        in_specs=[
            pl.BlockSpec((None, CHUNK, SSD_WIDTH), seq),
            pl.BlockSpec((None, CHUNK, SSD_BC), lambda bi, d, i: (bi, chunk(d, i), SSD_WIDTH // SSD_BC)),
            pl.BlockSpec((None, None, CHUNK, LANES), rec),
            pl.BlockSpec((None, None, CHUNK, LANES), rec),
            pl.BlockSpec((None, CHUNK, RET_QK_WIDTH), seq),
            pl.BlockSpec((None, CHUNK, RET_QK_WIDTH), seq),
            pl.BlockSpec((None, CHUNK, RET_V_WIDTH), seq),
            pl.BlockSpec((None, RET_HEADS, LANES), lambda bi, d, i: (d, 0, 0)),
        ],
        out_specs=pl.BlockSpec((None, None, CHUNK, EV_MIX), rec),
        out_shape=jax.ShapeDtypeStruct((2, b, s, EV_MIX), BF16),
        scratch_shapes=[pltpu.VMEM((SSD_GROUPS, SSD_STATE, SSD_HPG * SSD_HEAD_DIM), F32),
                        pltpu.VMEM((RET_HEADS, RET_QK_DIM, RET_V_DIM), F32)],
        compiler_params=_params("parallel", "arbitrary", "arbitrary"),
        name="ev_ssd_ret",
    )(xbc_act, xbc_act, dt, dtt, q, k, v, rd)


def _ev_out_kernel(yf_ref, yb_ref, xs_ref, z_ref, g_ref, x_ref, dskip_ref, nssd_ref, ngn_ref, w_ref, npost_ref, o_ref):
    yf = yf_ref[...].astype(F32)
    yb = yb_ref[...].astype(F32)
    y = yf[:, :SSD_WIDTH] + yb[:, :SSD_WIDTH] + xs_ref[...].astype(F32) * dskip_ref[...]
    gg = y * _silu(z_ref[...].astype(F32))
    gw = SSD_WIDTH // SSD_GROUPS
    parts = []
    for g in range(SSD_GROUPS):
        part = gg[:, g * gw:(g + 1) * gw]
        part = part * lax.rsqrt(jnp.mean(part * part, axis=-1, keepdims=True) + NORM_EPS) * nssd_ref[:, g * gw:(g + 1) * gw]
        parts.append(part.astype(BF16))
    r = yf[:, SSD_WIDTH:] + yb[:, SSD_WIDTH:]
    gate = _silu(g_ref[...].astype(F32))
    for h in range(RET_HEADS):
        lanes = slice(h * RET_V_DIM, (h + 1) * RET_V_DIM)
        part = r[:, lanes]
        mu = jnp.mean(part, axis=-1, keepdims=True)
        cen = part - mu
        var = jnp.mean(cen * cen, axis=-1, keepdims=True)
        parts.append((cen * lax.rsqrt(var + NORM_EPS) * ngn_ref[:, lanes] * gate[:, lanes]).astype(BF16))
    mix = jnp.dot(jnp.concatenate(parts, axis=1), w_ref[...], preferred_element_type=F32)
    o_ref[...] = x_ref[...] + _rms(mix, npost_ref[...])


def _ev_out(y, xbc_act2d, z, g, x2d, dskip, nssd, ngn, w_out, npost):
    t = x2d.shape[0]
    tm = ROW_TILE
    row = lambda i: (i, 0)
    const = lambda i: (0, 0)
    return pl.pallas_call(
        _ev_out_kernel,
        grid=(t // tm,),
        in_specs=[
            pl.BlockSpec((None, tm, EV_MIX), lambda i: (0, i, 0)),
            pl.BlockSpec((None, tm, EV_MIX), lambda i: (1, i, 0)),
            pl.BlockSpec((tm, SSD_WIDTH), row),
            pl.BlockSpec((tm, SSD_WIDTH), row),
            pl.BlockSpec((tm, RET_V_WIDTH), row),
            pl.BlockSpec((tm, D_MODEL), row),
            pl.BlockSpec((1, SSD_WIDTH), const),
            pl.BlockSpec((1, SSD_WIDTH), const),
            pl.BlockSpec((1, RET_V_WIDTH), const),
            pl.BlockSpec((EV_MIX, D_MODEL), const),
            pl.BlockSpec((1, D_MODEL), const),
        ],
        out_specs=pl.BlockSpec((tm, D_MODEL), row),
        out_shape=jax.ShapeDtypeStruct((t, D_MODEL), F32),
        compiler_params=_params("parallel"),
        name="ev_out",
    )(y, y, xbc_act2d, z, g, x2d, dskip, nssd, ngn, w_out, npost)


FFN_HIDDEN_TILE = 1024


def _ffn_kernel(x_ref, npre_ref, w1_ref, w2_ref, npost_ref, o_ref):
    x = x_ref[...]
    hn = _rms(x, npre_ref[...]).astype(BF16)
    acc = jnp.zeros(x.shape, F32)
    for c in range(FFN_HIDDEN // FFN_HIDDEN_TILE):
        cols = slice(c * FFN_HIDDEN_TILE, (c + 1) * FFN_HIDDEN_TILE)
        a = jnp.maximum(jnp.dot(hn, w1_ref[:, cols], preferred_element_type=F32), 0.0)
        acc = acc + jnp.dot((a * a).astype(BF16), w2_ref[cols, :], preferred_element_type=F32)
    o_ref[...] = x + _rms(acc, npost_ref[...])


def _ffn(x2d, npre, w1, w2, npost):
    t = x2d.shape[0]
    tm = FFN_ROW_TILE
    row = lambda i: (i, 0)
    const = lambda i: (0, 0)
    return pl.pallas_call(
        _ffn_kernel,
        grid=(t // tm,),
        in_specs=[
            pl.BlockSpec((tm, D_MODEL), row),
            pl.BlockSpec((1, D_MODEL), const),
            pl.BlockSpec((D_MODEL, FFN_HIDDEN), const),
            pl.BlockSpec((FFN_HIDDEN, D_MODEL), const),
            pl.BlockSpec((1, D_MODEL), const),
        ],
        out_specs=pl.BlockSpec((tm, D_MODEL), row),
        out_shape=jax.ShapeDtypeStruct((t, D_MODEL), F32),
        compiler_params=_params("parallel"),
        name="ffn",
    )(x2d, npre, w1, w2, npost)


OD_Q0, OD_K0, OD_V0, OD_U0, OD_VG0, OD_END = 0, 1024, 2048, 3072, 3584, 4096


def _od_inproj_kernel(x_ref, nw_ref, w_ref, gnw_ref, cos_ref, sin_ref, qkv_ref, u_ref, vn_ref):
    hn = _rms(x_ref[...], nw_ref[...]).astype(BF16)

    def mm(c0, c1):
        return jnp.dot(hn, w_ref[:, c0:c1], preferred_element_type=F32)

    cos, sin = cos_ref[...], sin_ref[...]
    for j, t in enumerate(_rope_cols(mm(OD_Q0, OD_K0), cos, sin)):
        qkv_ref[:, OD_Q0 + j * LANES:OD_Q0 + (j + 1) * LANES] = (t * (ATT_HEAD_DIM ** -0.5)).astype(BF16)
    for j, t in enumerate(_rope_cols(mm(OD_K0, OD_V0), cos, sin)):
        qkv_ref[:, OD_K0 + j * LANES:OD_K0 + (j + 1) * LANES] = t.astype(BF16)
    qkv_ref[:, OD_V0:OD_U0] = mm(OD_V0, OD_U0).astype(BF16)
    u_ref[...] = mm(OD_U0, OD_VG0).astype(BF16)
    vg = mm(OD_VG0, OD_END)
    mu = jnp.mean(vg, axis=-1, keepdims=True)
    cen = vg - mu
    var = jnp.mean(cen * cen, axis=-1, keepdims=True)
    vn_ref[...] = (cen * lax.rsqrt(var + NORM_EPS) * gnw_ref[...]).astype(BF16)


def _od_inproj(x2d, seq, nw, w_in, gnw, cos, sin):
    t = x2d.shape[0]
    tm = ROW_TILE
    nseq = seq // tm
    row = lambda i: (i, 0)
    const = lambda i: (0, 0)
    return pl.pallas_call(
        _od_inproj_kernel,
        grid=(t // tm,),
        in_specs=[
            pl.BlockSpec((tm, D_MODEL), row),
            pl.BlockSpec((1, D_MODEL), const),
            pl.BlockSpec(w_in.shape, const),
            pl.BlockSpec((1, GMLP_WIDTH), const),
            pl.BlockSpec((tm, LANES), lambda i: (i % nseq, 0)),
            pl.BlockSpec((tm, LANES), lambda i: (i % nseq, 0)),
        ],
        out_specs=[
            pl.BlockSpec((tm, 3 * ATT_WIDTH), row),
            pl.BlockSpec((tm, GMLP_WIDTH), row),
            pl.BlockSpec((tm, GMLP_WIDTH), row),
        ],
        out_shape=[jax.ShapeDtypeStruct((t, 3 * ATT_WIDTH), BF16), jax.ShapeDtypeStruct((t, GMLP_WIDTH), BF16),
                   jax.ShapeDtypeStruct((t, GMLP_WIDTH), BF16)],
        compiler_params=_params("parallel"),
        name="od_inproj",
    )(x2d, nw, w_in, gnw, cos, sin)


ATT_Q_TILE = 128


def _attn_kernel(q_ref, kp_ref, kc_ref, kn_ref, vp_ref, vc_ref, vn_ref, o_ref, lse_ref):
    i = pl.program_id(2)
    last = pl.num_programs(2) - 1
    nk = ATT_Q_TILE + 2 * ATT_HALF
    qq = lax.broadcasted_iota(jnp.int32, (ATT_Q_TILE, nk), 0)
    kk = lax.broadcasted_iota(jnp.int32, (ATT_Q_TILE, nk), 1)
    k_lo = jnp.where(i > 0, 0, ATT_HALF)
    k_hi = jnp.where(i < last, nk - 1, ATT_HALF + ATT_Q_TILE - 1)
    valid = (kk >= jnp.maximum(qq, k_lo)) & (kk <= jnp.minimum(qq + 2 * ATT_HALF, k_hi))
    k_all = jnp.concatenate([kp_ref[...], kc_ref[...], kn_ref[...]], axis=0)
    v_all = jnp.concatenate([vp_ref[...], vc_ref[...], vn_ref[...]], axis=0)
    lane = lax.broadcasted_iota(jnp.int32, (ATT_Q_TILE, LANES), 1)
    lse_all = jnp.zeros((ATT_Q_TILE, LANES), F32)
    for h in range(ATT_HEADS):
        lanes = slice(h * ATT_HEAD_DIM, (h + 1) * ATT_HEAD_DIM)
        s = lax.dot_general(q_ref[:, lanes], k_all[:, lanes], (((1,), (1,)), ((), ())), preferred_element_type=F32)
        s = jnp.where(valid, s, NEG_BIG)
        m = jnp.max(s, axis=-1, keepdims=True)
        p = jnp.exp(s - m)
        den = jnp.sum(p, axis=-1, keepdims=True)
        o = jnp.dot(p.astype(BF16), v_all[:, lanes], preferred_element_type=F32)
        o_ref[:, lanes] = (o / den).astype(BF16)
        lse_all = jnp.where(lane == h, m + jnp.log(den), lse_all)
    lse_ref[...] = lse_all


def _attn(qkv3d, dilation):
    b, s, _ = qkv3d.shape
    r = dilation
    l = s // r
    tq = ATT_Q_TILE
    per = tq // ATT_HALF
    nhalf = l // ATT_HALF
    qkv = jnp.transpose(qkv3d.reshape(b, l, r, 3 * ATT_WIDTH), (0, 2, 1, 3))
    main = lambda col: pl.BlockSpec((None, None, tq, ATT_WIDTH), lambda bi, c, i: (bi, c, i, col))
    prev = lambda col: pl.BlockSpec((None, None, ATT_HALF, ATT_WIDTH),
                                    lambda bi, c, i: (bi, c, jnp.maximum(i * per - 1, 0), col))
    nxt = lambda col: pl.BlockSpec((None, None, ATT_HALF, ATT_WIDTH),
                                   lambda bi, c, i: (bi, c, jnp.minimum((i + 1) * per, nhalf - 1), col))
    o, lse = pl.pallas_call(
        _attn_kernel,
        grid=(b, r, l // tq),
        in_specs=[main(0), prev(1), main(1), nxt(1), prev(2), main(2), nxt(2)],
        out_specs=[
            pl.BlockSpec((None, None, tq, ATT_WIDTH), lambda bi, c, i: (bi, c, i, 0)),
            pl.BlockSpec((None, None, tq, LANES), lambda bi, c, i: (bi, c, i, 0)),
        ],
        out_shape=[jax.ShapeDtypeStruct((b, r, l, ATT_WIDTH), BF16), jax.ShapeDtypeStruct((b, r, l, LANES), F32)],
        compiler_params=_params("parallel", "parallel", "parallel"),
        name=f"od_attn_r{r}",
    )(qkv, qkv, qkv, qkv, qkv, qkv, qkv)
    o = jnp.transpose(o, (0, 2, 1, 3)).reshape(b * s, ATT_WIDTH)
    lse = jnp.transpose(lse, (0, 2, 1, 3)).reshape(b * s, LANES)
    return o, lse


def _od_out_kernel(o1_ref, o2_ref, o3_ref, l1_ref, l2_ref, l3_ref, u_ref, vn_ref, ws_ref, bs_ref, x_ref, w_ref,
                   npost_ref, out_ref, cat_ref):
    l1, l2, l3 = l1_ref[...], l2_ref[...], l3_ref[...]
    m = jnp.maximum(jnp.maximum(l1, l2), l3)
    e1, e2, e3 = jnp.exp(l1 - m), jnp.exp(l2 - m), jnp.exp(l3 - m)
    inv = 1.0 / (e1 + e2 + e3)
    w1, w2, w3 = e1 * inv, e2 * inv, e3 * inv
    for h in range(ATT_HEADS):
        lanes = slice(h * ATT_HEAD_DIM, (h + 1) * ATT_HEAD_DIM)
        att = (w1[:, h:h + 1] * o1_ref[:, lanes].astype(F32) + w2[:, h:h + 1] * o2_ref[:, lanes].astype(F32)
               + w3[:, h:h + 1] * o3_ref[:, lanes].astype(F32))
        cat_ref[:, lanes] = att.astype(BF16)
    for g in range(GMLP_GROUPS):
        lanes = slice(g * GMLP_GROUP_DIM, (g + 1) * GMLP_GROUP_DIM)
        mixed = jnp.dot(ws_ref[g], vn_ref[:, lanes], preferred_element_type=F32) + bs_ref[:, lanes]
        sg = u_ref[:, lanes].astype(F32) * mixed
        cat_ref[:, ATT_WIDTH + g * GMLP_GROUP_DIM:ATT_WIDTH + (g + 1) * GMLP_GROUP_DIM] = sg.astype(BF16)
    mix = jnp.dot(cat_ref[...], w_ref[...], preferred_element_type=F32)
    out_ref[...] = x_ref[...] + _rms(mix, npost_ref[...])


def _od_out(o1, o2, o3, l1, l2, l3, u, vn, ws, bs, x2d, w_out, npost):
    t = x2d.shape[0]
    tm = CHUNK
    row = lambda i: (i, 0)
    const = lambda i: (0, 0)
    return pl.pallas_call(
        _od_out_kernel,
        grid=(t // tm,),
        in_specs=[
            pl.BlockSpec((tm, ATT_WIDTH), row), pl.BlockSpec((tm, ATT_WIDTH), row), pl.BlockSpec((tm, ATT_WIDTH), row),
            pl.BlockSpec((tm, LANES), row), pl.BlockSpec((tm, LANES), row), pl.BlockSpec((tm, LANES), row),
            pl.BlockSpec((tm, GMLP_WIDTH), row), pl.BlockSpec((tm, GMLP_WIDTH), row),
            pl.BlockSpec((GMLP_GROUPS, CHUNK, CHUNK), lambda i: (0, 0, 0)),
            pl.BlockSpec((CHUNK, GMLP_WIDTH), const),
            pl.BlockSpec((tm, D_MODEL), row),
            pl.BlockSpec((OD_MIX, D_MODEL), const),
            pl.BlockSpec((1, D_MODEL), const),
        ],
        out_specs=pl.BlockSpec((tm, D_MODEL), row),
        out_shape=jax.ShapeDtypeStruct((t, D_MODEL), F32),
        scratch_shapes=[pltpu.VMEM((tm, OD_MIX), BF16)],
        compiler_params=_params("parallel"),
        name="od_out",
    )(o1, o2, o3, l1, l2, l3, u, vn, ws, bs, x2d, w_out, npost)


def _rope_tables(seq):
    inv_freq = ROPE_THETA ** (-jnp.arange(0, ATT_HEAD_DIM, 2, dtype=F32) / ATT_HEAD_DIM)
    ang = jnp.arange(seq, dtype=F32)[:, None] * inv_freq[None, :]
    ang = jnp.concatenate([ang, ang, ang, ang], axis=-1)
    sign = jnp.where((jnp.arange(LANES) % 64) < 32, -1.0, 1.0).astype(F32)
    return jnp.cos(ang), jnp.sin(ang) * sign[None, :]


def _dt_lanes(per_dir):
    fwd = jnp.tile(per_dir[0], 3)
    bwd = jnp.tile(per_dir[1], 3)
    pad = jnp.zeros((64 - 3 * SSD_HEADS,), F32)
    return jnp.concatenate([fwd, pad, bwd, pad])[None, :]


def _prep_even(ev_in_proj, ev_conv_w, ev_conv_b, ssd_dt_bias, ssd_a_log, ssd_d, ssd_norm_w, ret_decay, ret_gn_w, ev_out_proj):
    wz, wxbc, wdt, wq, wk, wv, wg = jnp.split(ev_in_proj, [1024, 2560, 2592, 3104, 3616, 4640], axis=-1)
    w_main = jnp.concatenate([wz, wxbc, wq, wk, wv, wg], axis=-1).astype(BF16)
    wdt_f, wdt_b = wdt[:, :SSD_HEADS], wdt[:, SSD_HEADS:]
    zpad = jnp.zeros((D_MODEL, 64 - 3 * SSD_HEADS), F32)
    w_dt = jnp.concatenate([wdt_f, wdt_f, wdt_f, zpad, wdt_b, wdt_b, wdt_b, zpad], axis=-1).astype(BF16)
    conv_w = jnp.concatenate([ev_conv_w, jnp.zeros((8 - SSD_CONV, SSD_CONV_CH), F32)], axis=0)
    return dict(
        w_main=w_main, w_dt=w_dt, dt_bias=_dt_lanes(ssd_dt_bias), a_log=_dt_lanes(ssd_a_log),
        conv_w=conv_w, conv_b=ev_conv_b[None, :],
        dskip=jnp.repeat(ssd_d, SSD_HEAD_DIM)[None, :], nssd=ssd_norm_w[None, :], ngn=ret_gn_w[None, :],
        rd=jnp.broadcast_to(ret_decay[:, :, None], (2, RET_HEADS, LANES)),
        w_out=ev_out_proj.astype(BF16),
    )


def _even_layer(x2d, b, s, p, nmix_pre, nmix_post, cos, sin):
    z, xbc, q, k, v, g, dt, dtt = _ev_inproj(x2d, s, nmix_pre, p["w_main"], p["w_dt"], p["dt_bias"], p["a_log"], cos, sin)
    xbc_act = _conv(xbc.reshape(b, s, SSD_CONV_CH), p["conv_w"], p["conv_b"])
    y = _ssd_ret(xbc_act, dt.reshape(2, b, s, LANES), dtt.reshape(2, b, s, LANES), q.reshape(b, s, -1),
                 k.reshape(b, s, -1), v.reshape(b, s, -1), p["rd"])
    return _ev_out(y.reshape(2, b * s, EV_MIX), xbc_act.reshape(b * s, SSD_CONV_CH), z, g, x2d, p["dskip"], p["nssd"],
                   p["ngn"], p["w_out"], nmix_post)


def _odd_layer(x2d, b, s, p, nmix_pre, nmix_post, cos, sin):
    qkv, u, vn = _od_inproj(x2d, s, nmix_pre, p["w_in"], p["gnw"], cos, sin)
    qkv3d = qkv.reshape(b, s, 3 * ATT_WIDTH)
    outs = [_attn(qkv3d, dilation) for _, dilation in DILATED_PATTERNS]
    (o1, l1), (o2, l2), (o3, l3) = outs
    return _od_out(o1, o2, o3, l1, l2, l3, u, vn, p["ws"], p["bs"], x2d, p["w_out"], nmix_post)


def _trunk(x, even, odd, norm_mix_pre, norm_mix_post, norm_ffn_pre, norm_ffn_post, w1, w2):
    b, s, _ = x.shape
    cos, sin = _rope_tables(s)
    x2d = x.reshape(b * s, D_MODEL)
    depth = norm_mix_pre.shape[0]
    for i in range(depth):
        j = i // 2
        layer = _even_layer if i % 2 == 0 else _odd_layer
        params = even[j] if i % 2 == 0 else odd[j]
        x2d = layer(x2d, b, s, params, norm_mix_pre[i][None, :], norm_mix_post[i][None, :], cos, sin)
        x2d = _ffn(x2d, norm_ffn_pre[i][None, :], w1[i], w2[i], norm_ffn_post[i][None, :])
    return x2d.reshape(b, s, D_MODEL)


def kernel(x_prompt, x_sample, norm_mix_pre, norm_mix_post, norm_ffn_pre, norm_ffn_post, ffn_w1, ffn_w2, ev_in_proj, ev_conv_w, ev_conv_b, ssd_dt_bias, ssd_a_log, ssd_d, ssd_norm_w, ret_decay, ret_gn_w, ev_out_proj, od_in_proj, gmlp_norm_w, gmlp_ws, gmlp_bs, od_out_proj):
    even = [_prep_even(ev_in_proj[j], ev_conv_w[j], ev_conv_b[j], ssd_dt_bias[j], ssd_a_log[j], ssd_d[j], ssd_norm_w[j],
                       ret_decay[j], ret_gn_w[j], ev_out_proj[j]) for j in range(ev_in_proj.shape[0])]
    odd = [dict(w_in=od_in_proj[j].astype(BF16), gnw=gmlp_norm_w[j][None, :], ws=gmlp_ws[j].astype(BF16),
                bs=jnp.repeat(gmlp_bs[j].T, GMLP_GROUP_DIM, axis=1), w_out=od_out_proj[j].astype(BF16))
           for j in range(od_in_proj.shape[0])]
    w1 = ffn_w1.astype(BF16)
    w2 = ffn_w2.astype(BF16)
    run = functools.partial(_trunk, even=even, odd=odd, norm_mix_pre=norm_mix_pre, norm_mix_post=norm_mix_post,
                            norm_ffn_pre=norm_ffn_pre, norm_ffn_post=norm_ffn_post, w1=w1, w2=w2)
    return (run(x_prompt), run(x_sample))
```

```python
import functools
import math

import jax
import jax.numpy as jnp
from jax import lax
from jax.experimental import pallas as pl
from jax.experimental.pallas import tpu as pltpu

F32 = jnp.float32
BF16 = jnp.bfloat16

D_MODEL = 1024
NORM_EPS = 1e-6
ROPE_THETA = 10000.0
CHUNK = 128
LANES = 128

SSD_HEADS = 16
SSD_HEAD_DIM = 64
SSD_WIDTH = SSD_HEADS * SSD_HEAD_DIM
SSD_GROUPS = 2
SSD_HPG = SSD_HEADS // SSD_GROUPS
SSD_STATE = 128
SSD_CONV = 5
SSD_BC = 2 * SSD_GROUPS * SSD_STATE
SSD_CONV_CH = SSD_WIDTH + SSD_BC

RET_HEADS = 8
RET_QK_DIM = 64
RET_V_DIM = 128
RET_QK_WIDTH = RET_HEADS * RET_QK_DIM
RET_V_WIDTH = RET_HEADS * RET_V_DIM
EV_MIX = SSD_WIDTH + RET_V_WIDTH

ATT_HEADS = 16
ATT_HEAD_DIM = 64
ATT_WIDTH = ATT_HEADS * ATT_HEAD_DIM
DILATED_PATTERNS = ((128, 1), (512, 4), (2048, 16))
ATT_HALF = 64

GMLP_GROUPS = 8
GMLP_GROUP_DIM = 64
GMLP_WIDTH = GMLP_GROUPS * GMLP_GROUP_DIM
OD_MIX = ATT_WIDTH + GMLP_WIDTH
FFN_HIDDEN = 4 * D_MODEL

EV_Z0, EV_XBC0, EV_Q0, EV_K0, EV_V0, EV_G0, EV_END = 0, 1024, 2560, 3072, 3584, 4608, 5632
DT_LANE, CS_LANE, TOT_LANE = 0, 16, 32

ROW_TILE = 256
FFN_ROW_TILE = 512
VMEM_LIMIT = 56 * 1024 * 1024
NEG_BIG = -1e30
LOG2E = 1.4426950408889634


def _params(*sem):
    return pltpu.CompilerParams(dimension_semantics=sem, vmem_limit_bytes=VMEM_LIMIT)


def _rms(x, w):
    return x * lax.rsqrt(jnp.mean(x * x, axis=-1, keepdims=True) + NORM_EPS) * w


def _silu(x):
    return x / (1.0 + jnp.exp(-x))


def _softplus(x):
    return jnp.maximum(x, 0.0) + jnp.log1p(jnp.exp(-jnp.abs(x)))


def _rope(t, cos, sin_signed, first_half):
    rot = jnp.where(first_half, pltpu.roll(t, 96, 1), pltpu.roll(t, 32, 1))
    return t * cos + rot * sin_signed


def _rope_cols(a, cos, sin_signed):
    lane = lax.broadcasted_iota(jnp.int32, (a.shape[0], LANES), 1)
    first_half = (lane % 64) < 32
    return [_rope(a[:, j * LANES:(j + 1) * LANES], cos, sin_signed, first_half) for j in range(a.shape[1] // LANES)]


def _split3(x):
    hi = x.astype(BF16)
    r1 = x - hi.astype(F32)
    mid = r1.astype(BF16)
    lo = (r1 - mid.astype(F32)).astype(BF16)
    return hi, mid, lo


def _ev_inproj_kernel(x_ref, nw_ref, w_ref, wdt_ref, dtb_ref, alog_ref, cos_ref, sin_ref,
                      z_ref, xbc_ref, q_ref, k_ref, v_ref, g_ref, dt_ref, dtt_ref):
    tm = x_ref.shape[0]
    hn = _rms(x_ref[...], nw_ref[...]).astype(BF16)

    def mm(c0, c1):
        return jnp.dot(hn, w_ref[:, c0:c1], preferred_element_type=F32)

    z_ref[...] = mm(EV_Z0, EV_XBC0).astype(BF16)
    xbc_ref[...] = mm(EV_XBC0, EV_Q0).astype(BF16)
    cos, sin = cos_ref[...], sin_ref[...]
    for j, t in enumerate(_rope_cols(mm(EV_Q0, EV_K0), cos, sin)):
        q_ref[:, j * LANES:(j + 1) * LANES] = t.astype(BF16)
    for j, t in enumerate(_rope_cols(mm(EV_K0, EV_V0), cos, sin)):
        k_ref[:, j * LANES:(j + 1) * LANES] = (t * (RET_QK_DIM ** -0.5)).astype(BF16)
    v_ref[...] = mm(EV_V0, EV_G0).astype(BF16)
    g_ref[...] = mm(EV_G0, EV_END).astype(BF16)

    raw = jnp.dot(hn, wdt_ref[...], preferred_element_type=F32)
    dt = _softplus(raw + dtb_ref[...])
    la = dt * (-jnp.exp(alog_ref[...]))
    lane = lax.broadcasted_iota(jnp.int32, (CHUNK, LANES), 1)
    sub = lane % 64
    is_dt = sub < CS_LANE
    is_cs = (sub >= CS_LANE) & (sub < TOT_LANE)
    is_bwd = lane >= 64
    row = lax.broadcasted_iota(jnp.int32, (CHUNK, CHUNK), 0)
    col = lax.broadcasted_iota(jnp.int32, (CHUNK, CHUNK), 1)
    tri = jnp.where(row >= col, 1.0, 0.0).astype(BF16)
    for c in range(tm // CHUNK):
        rows = slice(c * CHUNK, (c + 1) * CHUNK)
        la_c = la[rows]
        cs = sum(jnp.dot(tri, part, preferred_element_type=F32) for part in _split3(la_c))
        tot = jnp.broadcast_to(cs[CHUNK - 1:CHUNK, :], cs.shape)
        rcs = tot - cs + la_c
        rec = jnp.where(is_dt, dt[rows], jnp.where(is_cs, jnp.where(is_bwd, rcs, cs), tot))
        rec_b = pltpu.roll(rec, 64, 1)
        dt_ref[0, rows, :] = rec
        dt_ref[1, rows, :] = rec_b
        dtt_ref[0, rows, :] = rec.T
        dtt_ref[1, rows, :] = rec_b.T


def _ev_inproj(x2d, seq, nw, w_main, w_dt, dt_bias, a_log, cos, sin):
    t = x2d.shape[0]
    tm = ROW_TILE
    nseq = seq // tm
    row = lambda i: (i, 0)
    const = lambda i: (0, 0)
    bf = lambda n: jax.ShapeDtypeStruct((t, n), BF16)
    return pl.pallas_call(
        _ev_inproj_kernel,
        grid=(t // tm,),
        in_specs=[
            pl.BlockSpec((tm, D_MODEL), row),
            pl.BlockSpec((1, D_MODEL), const),
            pl.BlockSpec(w_main.shape, const),
            pl.BlockSpec(w_dt.shape, const),
            pl.BlockSpec((1, LANES), const),
            pl.BlockSpec((1, LANES), const),
            pl.BlockSpec((tm, LANES), lambda i: (i % nseq, 0)),
            pl.BlockSpec((tm, LANES), lambda i: (i % nseq, 0)),
        ],
        out_specs=[
            pl.BlockSpec((tm, SSD_WIDTH), row),
            pl.BlockSpec((tm, SSD_CONV_CH), row),
            pl.BlockSpec((tm, RET_QK_WIDTH), row),
            pl.BlockSpec((tm, RET_QK_WIDTH), row),
            pl.BlockSpec((tm, RET_V_WIDTH), row),
            pl.BlockSpec((tm, RET_V_WIDTH), row),
            pl.BlockSpec((2, tm, LANES), lambda i: (0, i, 0)),
            pl.BlockSpec((2, tm, LANES), lambda i: (0, i, 0)),
        ],
        out_shape=[bf(SSD_WIDTH), bf(SSD_CONV_CH), bf(RET_QK_WIDTH), bf(RET_QK_WIDTH), bf(RET_V_WIDTH), bf(RET_V_WIDTH),
                   jax.ShapeDtypeStruct((2, t, LANES), F32), jax.ShapeDtypeStruct((2, t, LANES), F32)],
        compiler_params=_params("parallel"),
        name="ev_inproj",
    )(x2d, nw, w_main, w_dt, dt_bias, a_log, cos, sin)


CONV_HALO = 16


def _conv_kernel(main_ref, prev_ref, next_ref, w_ref, b_ref, o_ref):
    i = pl.program_id(1)
    last = pl.num_programs(1) - 1
    tc = main_ref.shape[0]
    prev = jnp.where(i > 0, prev_ref[...].astype(F32), 0.0)
    nxt = jnp.where(i < last, next_ref[...].astype(F32), 0.0)
    ext = jnp.concatenate([prev, main_ref[...].astype(F32), nxt], axis=0)
    acc = jnp.broadcast_to(b_ref[...], (tc, SSD_CONV_CH))
    for j in range(SSD_CONV):
        off = CONV_HALO - SSD_CONV // 2 + j
        acc = acc + w_ref[j:j + 1, :] * ext[off:off + tc, :]
    o_ref[...] = _silu(acc).astype(BF16)


def _conv(xbc3d, conv_w, conv_b):
    b, s, c = xbc3d.shape
    tc = ROW_TILE
    per = tc // CONV_HALO
    nh = s // CONV_HALO
    return pl.pallas_call(
        _conv_kernel,
        grid=(b, s // tc),
        in_specs=[
            pl.BlockSpec((None, tc, c), lambda bi, i: (bi, i, 0)),
            pl.BlockSpec((None, CONV_HALO, c), lambda bi, i: (bi, jnp.maximum(i * per - 1, 0), 0)),
            pl.BlockSpec((None, CONV_HALO, c), lambda bi, i: (bi, jnp.minimum((i + 1) * per, nh - 1), 0)),
            pl.BlockSpec((8, c), lambda bi, i: (0, 0)),
            pl.BlockSpec((1, c), lambda bi, i: (0, 0)),
        ],
        out_specs=pl.BlockSpec((None, tc, c), lambda bi, i: (bi, i, 0)),
        out_shape=jax.ShapeDtypeStruct((b, s, c), BF16),
        compiler_params=_params("parallel", "parallel"),
        name="ev_conv",
    )(xbc3d, xbc3d, xbc3d, conv_w, conv_b)


def _ssd_ret_kernel(xs_ref, bc_ref, dt_ref, dtt_ref, q_ref, k_ref, v_ref, rd_ref, y_ref, h_ref, r_ref, dm_ref, dec_ref):
    d = pl.program_id(1)
    step = pl.program_id(2)
    fwd = d == 0
    row = lax.broadcasted_iota(jnp.int32, (CHUNK, CHUNK), 0)
    col = lax.broadcasted_iota(jnp.int32, (CHUNK, CHUNK), 1)
    ahead = jnp.where(fwd, row - col, col - row)
    causal = ahead >= 0
    first = col < SSD_HEAD_DIM
    first_row = first[0:1, :]
    m_lo = jnp.where(first_row, 1.0, 0.0).astype(BF16)
    m_hi = jnp.where(first_row, 0.0, 1.0).astype(BF16)

    @pl.when(step == 0)
    def _():
        h_ref[...] = jnp.zeros_like(h_ref)
        r_ref[...] = jnp.zeros_like(r_ref)
        rd = rd_ref[...]
        lg = jnp.minimum(rd, 0.0) - jnp.log1p(jnp.exp(-jnp.abs(rd)))
        rowf = row.astype(F32)
        dist = ahead.astype(F32)
        posq = jnp.where(fwd, rowf + 1.0, CHUNK - rowf)
        posk = jnp.where(fwd, CHUNK - 1.0 - rowf, rowf)
        for h in range(RET_HEADS):
            dm_ref[h] = jnp.where(causal, jnp.exp(lg[h:h + 1, :] * dist), 0.0)
        for pr in range(RET_HEADS // 2):
            lg_a, lg_b = lg[2 * pr:2 * pr + 1, :], lg[2 * pr + 1:2 * pr + 2, :]
            lg2 = jnp.where(first_row, lg_a, lg_b)
            dec_ref[0, pr] = jnp.exp(lg2 * posq)
            dec_ref[1, pr] = jnp.exp(lg2 * posk)
            dec_ref[2, pr] = jnp.exp(jnp.where(row < RET_QK_DIM, lg_a, lg_b) * float(CHUNK))

    nt = (((1,), (1,)), ((), ()))

    def ret_pair(pr):
        cols = slice(pr * LANES, (pr + 1) * LANES)
        q2 = q_ref[:, cols]
        k2 = k_ref[:, cols]
        rp = r_ref[pr]
        rpb = rp.astype(BF16)
        qd2 = (q2.astype(F32) * dec_ref[0, pr]).astype(BF16)
        kdt = (k2.astype(F32) * dec_ref[1, pr]).T.astype(BF16)
        st = []
        for half, msk in enumerate((m_lo, m_hi)):
            h = 2 * pr + half
            vh = v_ref[:, h * RET_V_DIM:(h + 1) * RET_V_DIM]
            sc = lax.dot_general(q2 * msk, k2, nt, preferred_element_type=F32) * dm_ref[h]
            y = (jnp.dot(sc.astype(BF16), vh, preferred_element_type=F32)
                 + jnp.dot(qd2 * msk, rpb, preferred_element_type=F32))
            y_ref[:, SSD_WIDTH + h * RET_V_DIM:SSD_WIDTH + (h + 1) * RET_V_DIM] = y.astype(BF16)
            st.append(jnp.dot(kdt[half * RET_QK_DIM:(half + 1) * RET_QK_DIM, :], vh, preferred_element_type=F32))
        r_ref[pr] = rp * dec_ref[2, pr] + jnp.concatenate(st, axis=0)

    rec2 = dt_ref[...] * LOG2E
    rect = dtt_ref[...]
    rect2 = rect * LOG2E

    def ssd_pair(g, pr, cb, hg, yoff, bgt):
        cols = slice((g * SSD_HPG // 2 + pr) * LANES, (g * SSD_HPG // 2 + pr + 1) * LANES)
        lanes = slice(pr * LANES, (pr + 1) * LANES)
        x2 = xs_ref[:, cols]
        yd, csb, st, dec = [], [], [], []
        for half in range(2):
            e = g * SSD_HPG + 2 * pr + half
            cs_row = rect2[CS_LANE + e:CS_LANE + e + 1, :]
            dt_row = rect[DT_LANE + e:DT_LANE + e + 1, :]
            tot_row = rect2[TOT_LANE + e:TOT_LANE + e + 1, :]
            csb.append(jnp.broadcast_to(rec2[:, CS_LANE + e:CS_LANE + e + 1], (CHUNK, CHUNK)))
            decay = jnp.where(causal, jnp.exp2(csb[half] - cs_row), 0.0)
            me = (cb * decay * dt_row).astype(BF16)
            yd.append(jnp.dot(me, x2, preferred_element_type=F32))
            bw = (bgt * (dt_row * jnp.exp2(tot_row - cs_row))).astype(BF16)
            st.append(jnp.dot(bw, x2, preferred_element_type=F32))
            dec.append(jnp.exp2(tot_row))
        y = jnp.where(first, yd[0], yd[1]) + yoff[:, lanes] * jnp.exp2(jnp.where(first, csb[0], csb[1]))
        y_ref[:, cols] = y.astype(BF16)
        h_ref[g, :, lanes] = hg[:, lanes] * jnp.where(first_row, dec[0], dec[1]) + jnp.where(first, st[0], st[1])

    for g in range(SSD_GROUPS):
        bg = bc_ref[:, g * SSD_STATE:(g + 1) * SSD_STATE]
        cg = bc_ref[:, (SSD_GROUPS + g) * SSD_STATE:(SSD_GROUPS + g + 1) * SSD_STATE]
        cb = lax.dot_general(cg, bg, nt, preferred_element_type=F32)
        hg = h_ref[g]
        yoff = jnp.dot(cg, hg.astype(BF16), preferred_element_type=F32)
        bgt = bg.astype(F32).T
        for pr in range(SSD_HPG // 2):
            if pr % 2 == 0:
                ret_pair(g * 2 + pr // 2)
            ssd_pair(g, pr, cb, hg, yoff, bgt)


def _ssd_ret(xbc_act, dt, dtt, q, k, v, rd):
    b, s, _ = xbc_act.shape
    nc = s // CHUNK

    def chunk(d, i):
        return i + d * (nc - 1 - 2 * i)

    seq = lambda bi, d, i: (bi, chunk(d, i), 0)
    rec = lambda bi, d, i: (d, bi, chunk(d, i), 0)
    return pl.pallas_call(
        _ssd_ret_kernel,
        grid=(b, 2, nc),
        in_specs=[
            pl.BlockSpec((None, CHUNK, SSD_WIDTH), seq),
            pl.BlockSpec((None, CHUNK, SSD_BC), lambda bi, d, i: (bi, chunk(d, i), SSD_WIDTH // SSD_BC)),
            pl.BlockSpec((None, None, CHUNK, LANES), rec),
            pl.BlockSpec((None, None, CHUNK, LANES), rec),
            pl.BlockSpec((None, CHUNK, RET_QK_WIDTH), seq),
            pl.BlockSpec((None, CHUNK, RET_QK_WIDTH), seq),
            pl.BlockSpec((None, CHUNK, RET_V_WIDTH), seq),
            pl.BlockSpec((None, RET_HEADS, LANES), lambda bi, d, i: (d, 0, 0)),
        ],
        out_specs=pl.BlockSpec((None, None, CHUNK, EV_MIX), rec),
        out_shape=jax.ShapeDtypeStruct((2, b, s, EV_MIX), BF16),
        scratch_shapes=[pltpu.VMEM((SSD_GROUPS, SSD_STATE, SSD_HPG * SSD_HEAD_DIM), F32),
                        pltpu.VMEM((RET_HEADS // 2, 2 * RET_QK_DIM, RET_V_DIM), F32),
                        pltpu.VMEM((RET_HEADS, CHUNK, CHUNK), F32),
                        pltpu.VMEM((3, RET_HEADS // 2, CHUNK, CHUNK), F32)],
        compiler_params=_params("parallel", "arbitrary", "arbitrary"),
        name="ev_ssd_ret",
    )(xbc_act, xbc_act, dt, dtt, q, k, v, rd)


def _ev_out_kernel(yf_ref, yb_ref, xs_ref, z_ref, g_ref, x_ref, dskip_ref, nssd_ref, ngn_ref, w_ref, npost_ref, o_ref):
    yf = yf_ref[...].astype(F32)
    yb = yb_ref[...].astype(F32)
    y = yf[:, :SSD_WIDTH] + yb[:, :SSD_WIDTH] + xs_ref[...].astype(F32) * dskip_ref[...]
    gg = y * _silu(z_ref[...].astype(F32))
    gw = SSD_WIDTH // SSD_GROUPS
    parts = []
    for g in range(SSD_GROUPS):
        part = gg[:, g * gw:(g + 1) * gw]
        part = part * lax.rsqrt(jnp.mean(part * part, axis=-1, keepdims=True) + NORM_EPS) * nssd_ref[:, g * gw:(g + 1) * gw]
        parts.append(part.astype(BF16))
    r = yf[:, SSD_WIDTH:] + yb[:, SSD_WIDTH:]
    gate = _silu(g_ref[...].astype(F32))
    for h in range(RET_HEADS):
        lanes = slice(h * RET_V_DIM, (h + 1) * RET_V_DIM)
        part = r[:, lanes]
        mu = jnp.mean(part, axis=-1, keepdims=True)
        cen = part - mu
        var = jnp.mean(cen * cen, axis=-1, keepdims=True)
        parts.append((cen * lax.rsqrt(var + NORM_EPS) * ngn_ref[:, lanes] * gate[:, lanes]).astype(BF16))
    mix = jnp.dot(jnp.concatenate(parts, axis=1), w_ref[...], preferred_element_type=F32)
    o_ref[...] = x_ref[...] + _rms(mix, npost_ref[...])


def _ev_out(y, xbc_act2d, z, g, x2d, dskip, nssd, ngn, w_out, npost):
    t = x2d.shape[0]
    tm = ROW_TILE
    row = lambda i: (i, 0)
    const = lambda i: (0, 0)
    return pl.pallas_call(
        _ev_out_kernel,
        grid=(t // tm,),
        in_specs=[
            pl.BlockSpec((None, tm, EV_MIX), lambda i: (0, i, 0)),
            pl.BlockSpec((None, tm, EV_MIX), lambda i: (1, i, 0)),
            pl.BlockSpec((tm, SSD_WIDTH), row),
            pl.BlockSpec((tm, SSD_WIDTH), row),
            pl.BlockSpec((tm, RET_V_WIDTH), row),
            pl.BlockSpec((tm, D_MODEL), row),
            pl.BlockSpec((1, SSD_WIDTH), const),
            pl.BlockSpec((1, SSD_WIDTH), const),
            pl.BlockSpec((1, RET_V_WIDTH), const),
            pl.BlockSpec((EV_MIX, D_MODEL), const),
            pl.BlockSpec((1, D_MODEL), const),
        ],
        out_specs=pl.BlockSpec((tm, D_MODEL), row),
        out_shape=jax.ShapeDtypeStruct((t, D_MODEL), F32),
        compiler_params=_params("parallel"),
        name="ev_out",
    )(y, y, xbc_act2d, z, g, x2d, dskip, nssd, ngn, w_out, npost)


FFN_HIDDEN_TILE = 1024


def _ffn_kernel(x_ref, npre_ref, w1_ref, w2_ref, npost_ref, o_ref):
    x = x_ref[...]
    hn = _rms(x, npre_ref[...]).astype(BF16)
    acc = jnp.zeros(x.shape, F32)
    for c in range(FFN_HIDDEN // FFN_HIDDEN_TILE):
        cols = slice(c * FFN_HIDDEN_TILE, (c + 1) * FFN_HIDDEN_TILE)
        a = jnp.maximum(jnp.dot(hn, w1_ref[:, cols], preferred_element_type=F32), 0.0)
        acc = acc + jnp.dot((a * a).astype(BF16), w2_ref[cols, :], preferred_element_type=F32)
    o_ref[...] = x + _rms(acc, npost_ref[...])


def _ffn(x2d, npre, w1, w2, npost):
    t = x2d.shape[0]
    tm = FFN_ROW_TILE
    row = lambda i: (i, 0)
    const = lambda i: (0, 0)
    return pl.pallas_call(
        _ffn_kernel,
        grid=(t // tm,),
        in_specs=[
            pl.BlockSpec((tm, D_MODEL), row),
            pl.BlockSpec((1, D_MODEL), const),
            pl.BlockSpec((D_MODEL, FFN_HIDDEN), const),
            pl.BlockSpec((FFN_HIDDEN, D_MODEL), const),
            pl.BlockSpec((1, D_MODEL), const),
        ],
        out_specs=pl.BlockSpec((tm, D_MODEL), row),
        out_shape=jax.ShapeDtypeStruct((t, D_MODEL), F32),
        compiler_params=_params("parallel"),
        name="ffn",
    )(x2d, npre, w1, w2, npost)


OD_Q0, OD_K0, OD_V0, OD_U0, OD_VG0, OD_END = 0, 1024, 2048, 3072, 3584, 4096


def _od_inproj_kernel(x_ref, nw_ref, w_ref, gnw_ref, cos_ref, sin_ref, qkv_ref, u_ref, vn_ref):
    hn = _rms(x_ref[...], nw_ref[...]).astype(BF16)

    def mm(c0, c1):
        return jnp.dot(hn, w_ref[:, c0:c1], preferred_element_type=F32)

    cos, sin = cos_ref[...], sin_ref[...]
    for j, t in enumerate(_rope_cols(mm(OD_Q0, OD_K0), cos, sin)):
        qkv_ref[:, OD_Q0 + j * LANES:OD_Q0 + (j + 1) * LANES] = (t * (ATT_HEAD_DIM ** -0.5)).astype(BF16)
    for j, t in enumerate(_rope_cols(mm(OD_K0, OD_V0), cos, sin)):
        qkv_ref[:, OD_K0 + j * LANES:OD_K0 + (j + 1) * LANES] = t.astype(BF16)
    qkv_ref[:, OD_V0:OD_U0] = mm(OD_V0, OD_U0).astype(BF16)
    u_ref[...] = mm(OD_U0, OD_VG0).astype(BF16)
    vg = mm(OD_VG0, OD_END)
    mu = jnp.mean(vg, axis=-1, keepdims=True)
    cen = vg - mu
    var = jnp.mean(cen * cen, axis=-1, keepdims=True)
    vn_ref[...] = (cen * lax.rsqrt(var + NORM_EPS) * gnw_ref[...]).astype(BF16)


def _od_inproj(x2d, seq, nw, w_in, gnw, cos, sin):
    t = x2d.shape[0]
    tm = ROW_TILE
    nseq = seq // tm
    row = lambda i: (i, 0)
    const = lambda i: (0, 0)
    return pl.pallas_call(
        _od_inproj_kernel,
        grid=(t // tm,),
        in_specs=[
            pl.BlockSpec((tm, D_MODEL), row),
            pl.BlockSpec((1, D_MODEL), const),
            pl.BlockSpec(w_in.shape, const),
            pl.BlockSpec((1, GMLP_WIDTH), const),
            pl.BlockSpec((tm, LANES), lambda i: (i % nseq, 0)),
            pl.BlockSpec((tm, LANES), lambda i: (i % nseq, 0)),
        ],
        out_specs=[
            pl.BlockSpec((tm, 3 * ATT_WIDTH), row),
            pl.BlockSpec((tm, GMLP_WIDTH), row),
            pl.BlockSpec((tm, GMLP_WIDTH), row),
        ],
        out_shape=[jax.ShapeDtypeStruct((t, 3 * ATT_WIDTH), BF16), jax.ShapeDtypeStruct((t, GMLP_WIDTH), BF16),
                   jax.ShapeDtypeStruct((t, GMLP_WIDTH), BF16)],
        compiler_params=_params("parallel"),
        name="od_inproj",
    )(x2d, nw, w_in, gnw, cos, sin)


ATT_Q_TILE = 128


def _attn_kernel(q_ref, kp_ref, kc_ref, kn_ref, vp_ref, vc_ref, vn_ref, o_ref, lse_ref):
    i = pl.program_id(2)
    last = pl.num_programs(2) - 1
    nk = ATT_Q_TILE + 2 * ATT_HALF
    qq = lax.broadcasted_iota(jnp.int32, (ATT_Q_TILE, nk), 0)
    kk = lax.broadcasted_iota(jnp.int32, (ATT_Q_TILE, nk), 1)
    k_lo = jnp.where(i > 0, 0, ATT_HALF)
    k_hi = jnp.where(i < last, nk - 1, ATT_HALF + ATT_Q_TILE - 1)
    valid = (kk >= jnp.maximum(qq, k_lo)) & (kk <= jnp.minimum(qq + 2 * ATT_HALF, k_hi))
    lane = lax.broadcasted_iota(jnp.int32, (ATT_Q_TILE, LANES), 1)
    first = lane < ATT_HEAD_DIM
    masks = (jnp.where(first[0:1, :], 1.0, 0.0).astype(BF16), jnp.where(first[0:1, :], 0.0, 1.0).astype(BF16))
    lse_all = jnp.zeros((ATT_Q_TILE, LANES), F32)
    for pr in range(ATT_HEADS // 2):
        cols = slice(pr * LANES, (pr + 1) * LANES)
        q2 = q_ref[:, cols]
        k2 = jnp.concatenate([kp_ref[:, cols], kc_ref[:, cols], kn_ref[:, cols]], axis=0)
        v2 = jnp.concatenate([vp_ref[:, cols], vc_ref[:, cols], vn_ref[:, cols]], axis=0)
        outs = []
        for half in range(2):
            s = lax.dot_general(q2 * masks[half], k2, (((1,), (1,)), ((), ())), preferred_element_type=F32)
            s = jnp.where(valid, s, NEG_BIG)
            m = jnp.max(s, axis=-1, keepdims=True)
            p = jnp.exp(s - m)
            den = jnp.sum(p, axis=-1, keepdims=True)
            outs.append(jnp.dot(p.astype(BF16), v2, preferred_element_type=F32) * (1.0 / den))
            lse_all = jnp.where(lane == 2 * pr + half, m + jnp.log(den), lse_all)
        o_ref[:, cols] = jnp.where(first, outs[0], outs[1]).astype(BF16)
    lse_ref[...] = lse_all


def _attn(qkv3d, dilation):
    b, s, _ = qkv3d.shape
    r = dilation
    l = s // r
    tq = ATT_Q_TILE
    per = tq // ATT_HALF
    nhalf = l // ATT_HALF
    qkv = jnp.transpose(qkv3d.reshape(b, l, r, 3 * ATT_WIDTH), (0, 2, 1, 3))
    main = lambda col: pl.BlockSpec((None, None, tq, ATT_WIDTH), lambda bi, c, i: (bi, c, i, col))
    prev = lambda col: pl.BlockSpec((None, None, ATT_HALF, ATT_WIDTH),
                                    lambda bi, c, i: (bi, c, jnp.maximum(i * per - 1, 0), col))
    nxt = lambda col: pl.BlockSpec((None, None, ATT_HALF, ATT_WIDTH),
                                   lambda bi, c, i: (bi, c, jnp.minimum((i + 1) * per, nhalf - 1), col))
    o, lse = pl.pallas_call(
        _attn_kernel,
        grid=(b, r, l // tq),
        in_specs=[main(0), prev(1), main(1), nxt(1), prev(2), main(2), nxt(2)],
        out_specs=[
            pl.BlockSpec((None, None, tq, ATT_WIDTH), lambda bi, c, i: (bi, c, i, 0)),
            pl.BlockSpec((None, None, tq, LANES), lambda bi, c, i: (bi, c, i, 0)),
        ],
        out_shape=[jax.ShapeDtypeStruct((b, r, l, ATT_WIDTH), BF16), jax.ShapeDtypeStruct((b, r, l, LANES), F32)],
        compiler_params=_params("parallel", "parallel", "parallel"),
        name=f"od_attn_r{r}",
    )(qkv, qkv, qkv, qkv, qkv, qkv, qkv)
    o = jnp.transpose(o, (0, 2, 1, 3)).reshape(b * s, ATT_WIDTH)
    lse = jnp.transpose(lse, (0, 2, 1, 3)).reshape(b * s, LANES)
    return o, lse


def _od_out_kernel(o1_ref, o2_ref, o3_ref, l1_ref, l2_ref, l3_ref, u_ref, vn_ref, ws_ref, bs_ref, x_ref, w_ref,
                   npost_ref, ex_ref, out_ref):
    tm = x_ref.shape[0]
    l1, l2, l3 = l1_ref[...], l2_ref[...], l3_ref[...]
    m = jnp.maximum(jnp.maximum(l1, l2), l3)
    e1, e2, e3 = jnp.exp(l1 - m), jnp.exp(l2 - m), jnp.exp(l3 - m)
    inv = 1.0 / (e1 + e2 + e3)

    def per_head_lanes(w):
        hi = w.astype(BF16)
        lo = (w - hi.astype(F32)).astype(BF16)
        return (jnp.dot(hi, ex_ref[...], preferred_element_type=F32) + jnp.dot(lo, ex_ref[...], preferred_element_type=F32))

    o3 = o3_ref[...].astype(F32)
    att = (o3 + per_head_lanes(e1 * inv) * (o1_ref[...].astype(F32) - o3)
           + per_head_lanes(e2 * inv) * (o2_ref[...].astype(F32) - o3))
    first = lax.broadcasted_iota(jnp.int32, (CHUNK, LANES), 1) < GMLP_GROUP_DIM
    gate_rows = []
    for c in range(tm // CHUNK):
        rows = slice(c * CHUNK, (c + 1) * CHUNK)
        parts = []
        for pr in range(GMLP_GROUPS // 2):
            cols = slice(pr * LANES, (pr + 1) * LANES)
            vn2 = vn_ref[rows, cols]
            mixed = jnp.where(first, jnp.dot(ws_ref[2 * pr], vn2, preferred_element_type=F32),
                              jnp.dot(ws_ref[2 * pr + 1], vn2, preferred_element_type=F32)) + bs_ref[:, cols]
            parts.append((u_ref[rows, cols].astype(F32) * mixed).astype(BF16))
        gate_rows.append(jnp.concatenate(parts, axis=1))
    cat = jnp.concatenate([att.astype(BF16), jnp.concatenate(gate_rows, axis=0)], axis=1)
    mix = jnp.dot(cat, w_ref[...], preferred_element_type=F32)
    out_ref[...] = x_ref[...] + _rms(mix, npost_ref[...])


def _od_out(o1, o2, o3, l1, l2, l3, u, vn, ws, bs, x2d, w_out, npost):
    t = x2d.shape[0]
    tm = ROW_TILE
    row = lambda i: (i, 0)
    const = lambda i: (0, 0)
    head_of_lane = jnp.arange(ATT_WIDTH) // ATT_HEAD_DIM
    expand = (jnp.arange(LANES)[:, None] == head_of_lane[None, :]).astype(BF16)
    return pl.pallas_call(
        _od_out_kernel,
        grid=(t // tm,),
        in_specs=[
            pl.BlockSpec((tm, ATT_WIDTH), row), pl.BlockSpec((tm, ATT_WIDTH), row), pl.BlockSpec((tm, ATT_WIDTH), row),
            pl.BlockSpec((tm, LANES), row), pl.BlockSpec((tm, LANES), row), pl.BlockSpec((tm, LANES), row),
            pl.BlockSpec((tm, GMLP_WIDTH), row), pl.BlockSpec((tm, GMLP_WIDTH), row),
            pl.BlockSpec((GMLP_GROUPS, CHUNK, CHUNK), lambda i: (0, 0, 0)),
            pl.BlockSpec((CHUNK, GMLP_WIDTH), const),
            pl.BlockSpec((tm, D_MODEL), row),
            pl.BlockSpec((OD_MIX, D_MODEL), const),
            pl.BlockSpec((1, D_MODEL), const),
            pl.BlockSpec((LANES, ATT_WIDTH), const),
        ],
        out_specs=pl.BlockSpec((tm, D_MODEL), row),
        out_shape=jax.ShapeDtypeStruct((t, D_MODEL), F32),
        compiler_params=_params("parallel"),
        name="od_out",
    )(o1, o2, o3, l1, l2, l3, u, vn, ws, bs, x2d, w_out, npost, expand)


def _rope_tables(seq):
    inv_freq = ROPE_THETA ** (-jnp.arange(0, ATT_HEAD_DIM, 2, dtype=F32) / ATT_HEAD_DIM)
    ang = jnp.arange(seq, dtype=F32)[:, None] * inv_freq[None, :]
    ang = jnp.concatenate([ang, ang, ang, ang], axis=-1)
    sign = jnp.where((jnp.arange(LANES) % 64) < 32, -1.0, 1.0).astype(F32)
    return jnp.cos(ang), jnp.sin(ang) * sign[None, :]


def _dt_lanes(per_dir):
    fwd = jnp.tile(per_dir[0], 3)
    bwd = jnp.tile(per_dir[1], 3)
    pad = jnp.zeros((64 - 3 * SSD_HEADS,), F32)
    return jnp.concatenate([fwd, pad, bwd, pad])[None, :]


def _prep_even(ev_in_proj, ev_conv_w, ev_conv_b, ssd_dt_bias, ssd_a_log, ssd_d, ssd_norm_w, ret_decay, ret_gn_w, ev_out_proj):
    wz, wxbc, wdt, wq, wk, wv, wg = jnp.split(ev_in_proj, [1024, 2560, 2592, 3104, 3616, 4640], axis=-1)
    w_main = jnp.concatenate([wz, wxbc, wq, wk, wv, wg], axis=-1).astype(BF16)
    wdt_f, wdt_b = wdt[:, :SSD_HEADS], wdt[:, SSD_HEADS:]
    zpad = jnp.zeros((D_MODEL, 64 - 3 * SSD_HEADS), F32)
    w_dt = jnp.concatenate([wdt_f, wdt_f, wdt_f, zpad, wdt_b, wdt_b, wdt_b, zpad], axis=-1).astype(BF16)
    conv_w = jnp.concatenate([ev_conv_w, jnp.zeros((8 - SSD_CONV, SSD_CONV_CH), F32)], axis=0)
    return dict(
        w_main=w_main, w_dt=w_dt, dt_bias=_dt_lanes(ssd_dt_bias), a_log=_dt_lanes(ssd_a_log),
        conv_w=conv_w, conv_b=ev_conv_b[None, :],
        dskip=jnp.repeat(ssd_d, SSD_HEAD_DIM)[None, :], nssd=ssd_norm_w[None, :], ngn=ret_gn_w[None, :],
        rd=jnp.broadcast_to(ret_decay[:, :, None], (2, RET_HEADS, LANES)),
        w_out=ev_out_proj.astype(BF16),
    )


def _even_layer(x2d, b, s, p, nmix_pre, nmix_post, cos, sin):
    z, xbc, q, k, v, g, dt, dtt = _ev_inproj(x2d, s, nmix_pre, p["w_main"], p["w_dt"], p["dt_bias"], p["a_log"], cos, sin)
    xbc_act = _conv(xbc.reshape(b, s, SSD_CONV_CH), p["conv_w"], p["conv_b"])
    y = _ssd_ret(xbc_act, dt.reshape(2, b, s, LANES), dtt.reshape(2, b, s, LANES), q.reshape(b, s, -1),
                 k.reshape(b, s, -1), v.reshape(b, s, -1), p["rd"])
    return _ev_out(y.reshape(2, b * s, EV_MIX), xbc_act.reshape(b * s, SSD_CONV_CH), z, g, x2d, p["dskip"], p["nssd"],
                   p["ngn"], p["w_out"], nmix_post)


def _odd_layer(x2d, b, s, p, nmix_pre, nmix_post, cos, sin):
    qkv, u, vn = _od_inproj(x2d, s, nmix_pre, p["w_in"], p["gnw"], cos, sin)
    qkv3d = qkv.reshape(b, s, 3 * ATT_WIDTH)
    outs = [_attn(qkv3d, dilation) for _, dilation in DILATED_PATTERNS]
    (o1, l1), (o2, l2), (o3, l3) = outs
    return _od_out(o1, o2, o3, l1, l2, l3, u, vn, p["ws"], p["bs"], x2d, p["w_out"], nmix_post)


def _trunk(x, even, odd, norm_mix_pre, norm_mix_post, norm_ffn_pre, norm_ffn_post, w1, w2):
    b, s, _ = x.shape
    cos, sin = _rope_tables(s)
    x2d = x.reshape(b * s, D_MODEL)
    depth = norm_mix_pre.shape[0]
    for i in range(depth):
        j = i // 2
        layer = _even_layer if i % 2 == 0 else _odd_layer
        params = even[j] if i % 2 == 0 else odd[j]
        x2d = layer(x2d, b, s, params, norm_mix_pre[i][None, :], norm_mix_post[i][None, :], cos, sin)
        x2d = _ffn(x2d, norm_ffn_pre[i][None, :], w1[i], w2[i], norm_ffn_post[i][None, :])
    return x2d.reshape(b, s, D_MODEL)


def kernel(x_prompt, x_sample, norm_mix_pre, norm_mix_post, norm_ffn_pre, norm_ffn_post, ffn_w1, ffn_w2, ev_in_proj, ev_conv_w, ev_conv_b, ssd_dt_bias, ssd_a_log, ssd_d, ssd_norm_w, ret_decay, ret_gn_w, ev_out_proj, od_in_proj, gmlp_norm_w, gmlp_ws, gmlp_bs, od_out_proj):
    even = [_prep_even(ev_in_proj[j], ev_conv_w[j], ev_conv_b[j], ssd_dt_bias[j], ssd_a_log[j], ssd_d[j], ssd_norm_w[j],
                       ret_decay[j], ret_gn_w[j], ev_out_proj[j]) for j in range(ev_in_proj.shape[0])]
    odd = [dict(w_in=od_in_proj[j].astype(BF16), gnw=gmlp_norm_w[j][None, :], ws=gmlp_ws[j].astype(BF16),
                bs=jnp.repeat(gmlp_bs[j].T, GMLP_GROUP_DIM, axis=1), w_out=od_out_proj[j].astype(BF16))
           for j in range(od_in_proj.shape[0])]
    w1 = ffn_w1.astype(BF16)
    w2 = ffn_w2.astype(BF16)
    run = functools.partial(_trunk, even=even, odd=odd, norm_mix_pre=norm_mix_pre, norm_mix_post=norm_mix_post,
                            norm_ffn_pre=norm_ffn_pre, norm_ffn_post=norm_ffn_post, w1=w1, w2=w2)
    return (run(x_prompt), run(x_sample))
```

```python
import functools
import math

import jax
import jax.numpy as jnp
from jax import lax
from jax.experimental import pallas as pl
from jax.experimental.pallas import tpu as pltpu

F32 = jnp.float32
BF16 = jnp.bfloat16

D_MODEL = 1024
NORM_EPS = 1e-6
ROPE_THETA = 10000.0
CHUNK = 128
LANES = 128

SSD_HEADS = 16
SSD_HEAD_DIM = 64
SSD_WIDTH = SSD_HEADS * SSD_HEAD_DIM
SSD_GROUPS = 2
SSD_HPG = SSD_HEADS // SSD_GROUPS
SSD_STATE = 128
SSD_CONV = 5
SSD_BC = 2 * SSD_GROUPS * SSD_STATE
SSD_CONV_CH = SSD_WIDTH + SSD_BC

RET_HEADS = 8
RET_QK_DIM = 64
RET_V_DIM = 128
RET_QK_WIDTH = RET_HEADS * RET_QK_DIM
RET_V_WIDTH = RET_HEADS * RET_V_DIM
EV_MIX = SSD_WIDTH + RET_V_WIDTH

ATT_HEADS = 16
ATT_HEAD_DIM = 64
ATT_WIDTH = ATT_HEADS * ATT_HEAD_DIM
DILATED_PATTERNS = ((128, 1), (512, 4), (2048, 16))
ATT_HALF = 64

GMLP_GROUPS = 8
GMLP_GROUP_DIM = 64
GMLP_WIDTH = GMLP_GROUPS * GMLP_GROUP_DIM
OD_MIX = ATT_WIDTH + GMLP_WIDTH
FFN_HIDDEN = 4 * D_MODEL

EV_Z0, EV_XBC0, EV_Q0, EV_K0, EV_V0, EV_G0, EV_END = 0, 1024, 2560, 3072, 3584, 4608, 5632
DT_LANE, CS_LANE, TOT_LANE = 0, 16, 32

ROW_TILE = 256
FFN_ROW_TILE = 512
VMEM_LIMIT = 56 * 1024 * 1024
NEG_BIG = -1e30
LOG2E = 1.4426950408889634
LN2 = 0.6931471805599453


def _params(*sem):
    return pltpu.CompilerParams(dimension_semantics=sem, vmem_limit_bytes=VMEM_LIMIT)


def _rms(x, w):
    return x * lax.rsqrt(jnp.mean(x * x, axis=-1, keepdims=True) + NORM_EPS) * w


def _silu(x):
    return x / (1.0 + jnp.exp(-x))


def _softplus(x):
    return jnp.maximum(x, 0.0) + jnp.log1p(jnp.exp(-jnp.abs(x)))


def _rope(t, cos, sin_signed, first_half):
    rot = jnp.where(first_half, pltpu.roll(t, 96, 1), pltpu.roll(t, 32, 1))
    return t * cos + rot * sin_signed


def _rope_cols(a, cos, sin_signed):
    lane = lax.broadcasted_iota(jnp.int32, (a.shape[0], LANES), 1)
    first_half = (lane % 64) < 32
    return [_rope(a[:, j * LANES:(j + 1) * LANES], cos, sin_signed, first_half) for j in range(a.shape[1] // LANES)]


def _split3(x):
    hi = x.astype(BF16)
    r1 = x - hi.astype(F32)
    mid = r1.astype(BF16)
    lo = (r1 - mid.astype(F32)).astype(BF16)
    return hi, mid, lo


def _ev_inproj_kernel(x_ref, nw_ref, w_ref, wdt_ref, dtb_ref, alog_ref, cos_ref, sin_ref,
                      z_ref, xbc_ref, q_ref, k_ref, v_ref, g_ref, dt_ref, dtt_ref):
    tm = x_ref.shape[0]
    hn = _rms(x_ref[...], nw_ref[...]).astype(BF16)

    def mm(c0, c1):
        return jnp.dot(hn, w_ref[:, c0:c1], preferred_element_type=F32)

    z_ref[...] = mm(EV_Z0, EV_XBC0).astype(BF16)
    xbc_ref[...] = mm(EV_XBC0, EV_Q0).astype(BF16)
    cos, sin = cos_ref[...], sin_ref[...]
    for j, t in enumerate(_rope_cols(mm(EV_Q0, EV_K0), cos, sin)):
        q_ref[:, j * LANES:(j + 1) * LANES] = t.astype(BF16)
    for j, t in enumerate(_rope_cols(mm(EV_K0, EV_V0), cos, sin)):
        k_ref[:, j * LANES:(j + 1) * LANES] = (t * (RET_QK_DIM ** -0.5)).astype(BF16)
    v_ref[...] = mm(EV_V0, EV_G0).astype(BF16)
    g_ref[...] = mm(EV_G0, EV_END).astype(BF16)

    raw = jnp.dot(hn, wdt_ref[...], preferred_element_type=F32)
    dt = _softplus(raw + dtb_ref[...])
    la = dt * (-jnp.exp(alog_ref[...]))
    lane = lax.broadcasted_iota(jnp.int32, (CHUNK, LANES), 1)
    sub = lane % 64
    is_dt = sub < CS_LANE
    is_cs = (sub >= CS_LANE) & (sub < TOT_LANE)
    is_bwd = lane >= 64
    row = lax.broadcasted_iota(jnp.int32, (CHUNK, CHUNK), 0)
    col = lax.broadcasted_iota(jnp.int32, (CHUNK, CHUNK), 1)
    tri = jnp.where(row >= col, 1.0, 0.0).astype(BF16)
    for c in range(tm // CHUNK):
        rows = slice(c * CHUNK, (c + 1) * CHUNK)
        la_c = la[rows]
        cs = sum(jnp.dot(tri, part, preferred_element_type=F32) for part in _split3(la_c))
        tot = jnp.broadcast_to(cs[CHUNK - 1:CHUNK, :], cs.shape)
        rcs = tot - cs + la_c
        rec = jnp.where(is_dt, dt[rows], jnp.where(is_cs, jnp.where(is_bwd, rcs, cs), tot))
        rec_b = pltpu.roll(rec, 64, 1)
        dt_ref[0, rows, :] = rec
        dt_ref[1, rows, :] = rec_b
        dtt_ref[0, rows, :] = rec.T
        dtt_ref[1, rows, :] = rec_b.T


def _ev_inproj(x2d, seq, nw, w_main, w_dt, dt_bias, a_log, cos, sin):
    t = x2d.shape[0]
    tm = ROW_TILE
    nseq = seq // tm
    row = lambda i: (i, 0)
    const = lambda i: (0, 0)
    bf = lambda n: jax.ShapeDtypeStruct((t, n), BF16)
    return pl.pallas_call(
        _ev_inproj_kernel,
        grid=(t // tm,),
        in_specs=[
            pl.BlockSpec((tm, D_MODEL), row),
            pl.BlockSpec((1, D_MODEL), const),
            pl.BlockSpec(w_main.shape, const),
            pl.BlockSpec(w_dt.shape, const),
            pl.BlockSpec((1, LANES), const),
            pl.BlockSpec((1, LANES), const),
            pl.BlockSpec((tm, LANES), lambda i: (i % nseq, 0)),
            pl.BlockSpec((tm, LANES), lambda i: (i % nseq, 0)),
        ],
        out_specs=[
            pl.BlockSpec((tm, SSD_WIDTH), row),
            pl.BlockSpec((tm, SSD_CONV_CH), row),
            pl.BlockSpec((tm, RET_QK_WIDTH), row),
            pl.BlockSpec((tm, RET_QK_WIDTH), row),
            pl.BlockSpec((tm, RET_V_WIDTH), row),
            pl.BlockSpec((tm, RET_V_WIDTH), row),
            pl.BlockSpec((2, tm, LANES), lambda i: (0, i, 0)),
            pl.BlockSpec((2, tm, LANES), lambda i: (0, i, 0)),
        ],
        out_shape=[bf(SSD_WIDTH), bf(SSD_CONV_CH), bf(RET_QK_WIDTH), bf(RET_QK_WIDTH), bf(RET_V_WIDTH), bf(RET_V_WIDTH),
                   jax.ShapeDtypeStruct((2, t, LANES), F32), jax.ShapeDtypeStruct((2, t, LANES), F32)],
        compiler_params=_params("parallel"),
        name="ev_inproj",
    )(x2d, nw, w_main, w_dt, dt_bias, a_log, cos, sin)


CONV_HALO = 16


def _conv_kernel(main_ref, prev_ref, next_ref, w_ref, b_ref, o_ref):
    i = pl.program_id(1)
    last = pl.num_programs(1) - 1
    tc = main_ref.shape[0]
    prev = jnp.where(i > 0, prev_ref[...].astype(F32), 0.0)
    nxt = jnp.where(i < last, next_ref[...].astype(F32), 0.0)
    ext = jnp.concatenate([prev, main_ref[...].astype(F32), nxt], axis=0)
    acc = jnp.broadcast_to(b_ref[...], (tc, SSD_CONV_CH))
    for j in range(SSD_CONV):
        off = CONV_HALO - SSD_CONV // 2 + j
        acc = acc + w_ref[j:j + 1, :] * ext[off:off + tc, :]
    o_ref[...] = _silu(acc).astype(BF16)


def _conv(xbc3d, conv_w, conv_b):
    b, s, c = xbc3d.shape
    tc = ROW_TILE
    per = tc // CONV_HALO
    nh = s // CONV_HALO
    return pl.pallas_call(
        _conv_kernel,
        grid=(b, s // tc),
        in_specs=[
            pl.BlockSpec((None, tc, c), lambda bi, i: (bi, i, 0)),
            pl.BlockSpec((None, CONV_HALO, c), lambda bi, i: (bi, jnp.maximum(i * per - 1, 0), 0)),
            pl.BlockSpec((None, CONV_HALO, c), lambda bi, i: (bi, jnp.minimum((i + 1) * per, nh - 1), 0)),
            pl.BlockSpec((8, c), lambda bi, i: (0, 0)),
            pl.BlockSpec((1, c), lambda bi, i: (0, 0)),
        ],
        out_specs=pl.BlockSpec((None, tc, c), lambda bi, i: (bi, i, 0)),
        out_shape=jax.ShapeDtypeStruct((b, s, c), BF16),
        compiler_params=_params("parallel", "parallel"),
        name="ev_conv",
    )(xbc3d, xbc3d, xbc3d, conv_w, conv_b)


def _ssd_ret_kernel(xs_ref, bc_ref, dt_ref, dtt_ref, q_ref, k_ref, v_ref, rd_ref, y_ref, h_ref, r_ref, dm_ref, dec_ref):
    d = pl.program_id(1)
    step = pl.program_id(2)
    fwd = d == 0
    row = lax.broadcasted_iota(jnp.int32, (CHUNK, CHUNK), 0)
    col = lax.broadcasted_iota(jnp.int32, (CHUNK, CHUNK), 1)
    ahead = jnp.where(fwd, row - col, col - row)
    causal = ahead >= 0
    first = col < SSD_HEAD_DIM
    first_row = first[0:1, :]
    m_lo = jnp.where(first_row, 1.0, 0.0).astype(BF16)
    m_hi = jnp.where(first_row, 0.0, 1.0).astype(BF16)

    @pl.when(step == 0)
    def _():
        h_ref[...] = jnp.zeros_like(h_ref)
        r_ref[...] = jnp.zeros_like(r_ref)
        rd = rd_ref[...]
        lg = jnp.minimum(rd, 0.0) - jnp.log1p(jnp.exp(-jnp.abs(rd)))
        rowf = row.astype(F32)
        dist = ahead.astype(F32)
        posq = jnp.where(fwd, rowf + 1.0, CHUNK - rowf)
        posk = jnp.where(fwd, CHUNK - 1.0 - rowf, rowf)
        for h in range(RET_HEADS):
            dm_ref[h] = jnp.where(causal, jnp.exp(lg[h:h + 1, :] * dist), 0.0)
        for pr in range(RET_HEADS // 2):
            lg_a, lg_b = lg[2 * pr:2 * pr + 1, :], lg[2 * pr + 1:2 * pr + 2, :]
            lg2 = jnp.where(first_row, lg_a, lg_b)
            dec_ref[0, pr] = jnp.exp(lg2 * posq)
            dec_ref[1, pr] = jnp.exp(lg2 * posk)
            dec_ref[2, pr] = jnp.exp(jnp.where(row < RET_QK_DIM, lg_a, lg_b) * float(CHUNK))

    nt = (((1,), (1,)), ((), ()))

    def ret_pair(pr):
        cols = slice(pr * LANES, (pr + 1) * LANES)
        q2 = q_ref[:, cols]
        k2 = k_ref[:, cols]
        rp = r_ref[pr]
        rpb = rp.astype(BF16)
        qd2 = (q2.astype(F32) * dec_ref[0, pr]).astype(BF16)
        kdt = (k2.astype(F32) * dec_ref[1, pr]).T.astype(BF16)
        st = []
        for half, msk in enumerate((m_lo, m_hi)):
            h = 2 * pr + half
            vh = v_ref[:, h * RET_V_DIM:(h + 1) * RET_V_DIM]
            sc = lax.dot_general(q2 * msk, k2, nt, preferred_element_type=F32) * dm_ref[h]
            y = (jnp.dot(sc.astype(BF16), vh, preferred_element_type=F32)
                 + jnp.dot(qd2 * msk, rpb, preferred_element_type=F32))
            y_ref[:, SSD_WIDTH + h * RET_V_DIM:SSD_WIDTH + (h + 1) * RET_V_DIM] = y.astype(BF16)
            st.append(jnp.dot(kdt[half * RET_QK_DIM:(half + 1) * RET_QK_DIM, :], vh, preferred_element_type=F32))
        r_ref[pr] = rp * dec_ref[2, pr] + jnp.concatenate(st, axis=0)

    rec2 = dt_ref[...] * LOG2E
    rect = dtt_ref[...]
    rect2 = rect * LOG2E

    def ssd_pair(g, pr, cb, hg, yoff, bgt):
        cols = slice((g * SSD_HPG // 2 + pr) * LANES, (g * SSD_HPG // 2 + pr + 1) * LANES)
        lanes = slice(pr * LANES, (pr + 1) * LANES)
        x2 = xs_ref[:, cols]
        yd, csb, st, dec = [], [], [], []
        for half in range(2):
            e = g * SSD_HPG + 2 * pr + half
            cs_row = rect2[CS_LANE + e:CS_LANE + e + 1, :]
            dt_row = rect[DT_LANE + e:DT_LANE + e + 1, :]
            tot_row = rect2[TOT_LANE + e:TOT_LANE + e + 1, :]
            csb.append(jnp.broadcast_to(rec2[:, CS_LANE + e:CS_LANE + e + 1], (CHUNK, CHUNK)))
            decay = jnp.where(causal, jnp.exp2(csb[half] - cs_row), 0.0)
            me = (cb * decay * dt_row).astype(BF16)
            yd.append(jnp.dot(me, x2, preferred_element_type=F32))
            bw = (bgt * (dt_row * jnp.exp2(tot_row - cs_row))).astype(BF16)
            st.append(jnp.dot(bw, x2, preferred_element_type=F32))
            dec.append(jnp.exp2(tot_row))
        y = jnp.where(first, yd[0], yd[1]) + yoff[:, lanes] * jnp.exp2(jnp.where(first, csb[0], csb[1]))
        y_ref[:, cols] = y.astype(BF16)
        h_ref[g, :, lanes] = hg[:, lanes] * jnp.where(first_row, dec[0], dec[1]) + jnp.where(first, st[0], st[1])

    for g in range(SSD_GROUPS):
        bg = bc_ref[:, g * SSD_STATE:(g + 1) * SSD_STATE]
        cg = bc_ref[:, (SSD_GROUPS + g) * SSD_STATE:(SSD_GROUPS + g + 1) * SSD_STATE]
        cb = lax.dot_general(cg, bg, nt, preferred_element_type=F32)
        hg = h_ref[g]
        yoff = jnp.dot(cg, hg.astype(BF16), preferred_element_type=F32)
        bgt = bg.astype(F32).T
        for pr in range(SSD_HPG // 2):
            if pr % 2 == 0:
                ret_pair(g * 2 + pr // 2)
            ssd_pair(g, pr, cb, hg, yoff, bgt)


def _ssd_ret(xbc_act, dt, dtt, q, k, v, rd):
    b, s, _ = xbc_act.shape
    nc = s // CHUNK

    def chunk(d, i):
        return i + d * (nc - 1 - 2 * i)

    seq = lambda bi, d, i: (bi, chunk(d, i), 0)
    rec = lambda bi, d, i: (d, bi, chunk(d, i), 0)
    return pl.pallas_call(
        _ssd_ret_kernel,
        grid=(b, 2, nc),
        in_specs=[
            pl.BlockSpec((None, CHUNK, SSD_WIDTH), seq),
            pl.BlockSpec((None, CHUNK, SSD_BC), lambda bi, d, i: (bi, chunk(d, i), SSD_WIDTH // SSD_BC)),
            pl.BlockSpec((None, None, CHUNK, LANES), rec),
            pl.BlockSpec((None, None, CHUNK, LANES), rec),
            pl.BlockSpec((None, CHUNK, RET_QK_WIDTH), seq),
            pl.BlockSpec((None, CHUNK, RET_QK_WIDTH), seq),
            pl.BlockSpec((None, CHUNK, RET_V_WIDTH), seq),
            pl.BlockSpec((None, RET_HEADS, LANES), lambda bi, d, i: (d, 0, 0)),
        ],
        out_specs=pl.BlockSpec((None, None, CHUNK, EV_MIX), rec),
        out_shape=jax.ShapeDtypeStruct((2, b, s, EV_MIX), BF16),
        scratch_shapes=[pltpu.VMEM((SSD_GROUPS, SSD_STATE, SSD_HPG * SSD_HEAD_DIM), F32),
                        pltpu.VMEM((RET_HEADS // 2, 2 * RET_QK_DIM, RET_V_DIM), F32),
                        pltpu.VMEM((RET_HEADS, CHUNK, CHUNK), F32),
                        pltpu.VMEM((3, RET_HEADS // 2, CHUNK, CHUNK), F32)],
        compiler_params=_params("parallel", "arbitrary", "arbitrary"),
        name="ev_ssd_ret",
    )(xbc_act, xbc_act, dt, dtt, q, k, v, rd)


def _ev_out_kernel(yf_ref, yb_ref, xs_ref, z_ref, g_ref, x_ref, dskip_ref, nssd_ref, ngn_ref, w_ref, npost_ref, o_ref):
    yf = yf_ref[...].astype(F32)
    yb = yb_ref[...].astype(F32)
    y = yf[:, :SSD_WIDTH] + yb[:, :SSD_WIDTH] + xs_ref[...].astype(F32) * dskip_ref[...]
    gg = y * _silu(z_ref[...].astype(F32))
    gw = SSD_WIDTH // SSD_GROUPS
    parts = []
    for g in range(SSD_GROUPS):
        part = gg[:, g * gw:(g + 1) * gw]
        part = part * lax.rsqrt(jnp.mean(part * part, axis=-1, keepdims=True) + NORM_EPS) * nssd_ref[:, g * gw:(g + 1) * gw]
        parts.append(part.astype(BF16))
    r = yf[:, SSD_WIDTH:] + yb[:, SSD_WIDTH:]
    gate = _silu(g_ref[...].astype(F32))
    for h in range(RET_HEADS):
        lanes = slice(h * RET_V_DIM, (h + 1) * RET_V_DIM)
        part = r[:, lanes]
        mu = jnp.mean(part, axis=-1, keepdims=True)
        cen = part - mu
        var = jnp.mean(cen * cen, axis=-1, keepdims=True)
        parts.append((cen * lax.rsqrt(var + NORM_EPS) * ngn_ref[:, lanes] * gate[:, lanes]).astype(BF16))
    mix = jnp.dot(jnp.concatenate(parts, axis=1), w_ref[...], preferred_element_type=F32)
    o_ref[...] = x_ref[...] + _rms(mix, npost_ref[...])


def _ev_out(y, xbc_act2d, z, g, x2d, dskip, nssd, ngn, w_out, npost):
    t = x2d.shape[0]
    tm = ROW_TILE
    row = lambda i: (i, 0)
    const = lambda i: (0, 0)
    return pl.pallas_call(
        _ev_out_kernel,
        grid=(t // tm,),
        in_specs=[
            pl.BlockSpec((None, tm, EV_MIX), lambda i: (0, i, 0)),
            pl.BlockSpec((None, tm, EV_MIX), lambda i: (1, i, 0)),
            pl.BlockSpec((tm, SSD_WIDTH), row),
            pl.BlockSpec((tm, SSD_WIDTH), row),
            pl.BlockSpec((tm, RET_V_WIDTH), row),
            pl.BlockSpec((tm, D_MODEL), row),
            pl.BlockSpec((1, SSD_WIDTH), const),
            pl.BlockSpec((1, SSD_WIDTH), const),
            pl.BlockSpec((1, RET_V_WIDTH), const),
            pl.BlockSpec((EV_MIX, D_MODEL), const),
            pl.BlockSpec((1, D_MODEL), const),
        ],
        out_specs=pl.BlockSpec((tm, D_MODEL), row),
        out_shape=jax.ShapeDtypeStruct((t, D_MODEL), F32),
        compiler_params=_params("parallel"),
        name="ev_out",
    )(y, y, xbc_act2d, z, g, x2d, dskip, nssd, ngn, w_out, npost)


FFN_HIDDEN_TILE = 1024


def _ffn_kernel(x_ref, npre_ref, w1_ref, w2_ref, npost_ref, o_ref):
    x = x_ref[...]
    hn = _rms(x, npre_ref[...]).astype(BF16)
    acc = jnp.zeros(x.shape, F32)
    for c in range(FFN_HIDDEN // FFN_HIDDEN_TILE):
        cols = slice(c * FFN_HIDDEN_TILE, (c + 1) * FFN_HIDDEN_TILE)
        a = jnp.maximum(jnp.dot(hn, w1_ref[:, cols], preferred_element_type=F32), 0.0)
        acc = acc + jnp.dot((a * a).astype(BF16), w2_ref[cols, :], preferred_element_type=F32)
    o_ref[...] = x + _rms(acc, npost_ref[...])


def _ffn(x2d, npre, w1, w2, npost):
    t = x2d.shape[0]
    tm = FFN_ROW_TILE
    row = lambda i: (i, 0)
    const = lambda i: (0, 0)
    return pl.pallas_call(
        _ffn_kernel,
        grid=(t // tm,),
        in_specs=[
            pl.BlockSpec((tm, D_MODEL), row),
            pl.BlockSpec((1, D_MODEL), const),
            pl.BlockSpec((D_MODEL, FFN_HIDDEN), const),
            pl.BlockSpec((FFN_HIDDEN, D_MODEL), const),
            pl.BlockSpec((1, D_MODEL), const),
        ],
        out_specs=pl.BlockSpec((tm, D_MODEL), row),
        out_shape=jax.ShapeDtypeStruct((t, D_MODEL), F32),
        compiler_params=_params("parallel"),
        name="ffn",
    )(x2d, npre, w1, w2, npost)


OD_Q0, OD_K0, OD_V0, OD_U0, OD_VG0, OD_END = 0, 1024, 2048, 3072, 3584, 4096


QKV_SLABS = 3 * ATT_WIDTH // LANES
DILATIONS = tuple(d for _, d in DILATED_PATTERNS)


def _od_inproj_kernel(x_ref, nw_ref, w_ref, gnw_ref, cos_ref, sin_ref, qkv_ref, qkv4_ref, qkv16_ref, u_ref, vn_ref,
                      slab_ref, slab4_ref):
    tm = x_ref.shape[0]
    hn = _rms(x_ref[...], nw_ref[...]).astype(BF16)

    def mm(c0, c1):
        return jnp.dot(hn, w_ref[:, c0:c1], preferred_element_type=F32)

    cos, sin = cos_ref[...], sin_ref[...]
    q_cols = [t * (ATT_HEAD_DIM ** -0.5 * LOG2E) for t in _rope_cols(mm(OD_Q0, OD_K0), cos, sin)]
    k_cols = _rope_cols(mm(OD_K0, OD_V0), cos, sin)
    v = mm(OD_V0, OD_U0)
    v_cols = [v[:, j * LANES:(j + 1) * LANES] for j in range(ATT_WIDTH // LANES)]
    for j, t in enumerate(q_cols + k_cols + v_cols):
        qkv_ref[:, j * LANES:(j + 1) * LANES] = t.astype(BF16)
        slab_ref[j] = t
    r4 = DILATIONS[1]
    n4 = tm // r4
    assert DILATIONS[2] == r4 * r4
    for j in range(QKV_SLABS):
        lanes = slice(j * LANES, (j + 1) * LANES)
        for c4 in range(r4):
            rows = slab_ref[j, pl.ds(c4, n4, stride=r4), :]
            qkv4_ref[c4, :, lanes] = rows.astype(BF16)
            slab4_ref[j, c4 * n4:(c4 + 1) * n4, :] = rows
        for c4 in range(r4):
            for c2 in range(r4):
                rows = slab4_ref[j, pl.ds(c4 * n4 + c2, n4 // r4, stride=r4), :]
                qkv16_ref[c4 + r4 * c2, :, lanes] = rows.astype(BF16)
    u_ref[...] = mm(OD_U0, OD_VG0).astype(BF16)
    vg = mm(OD_VG0, OD_END)
    mu = jnp.mean(vg, axis=-1, keepdims=True)
    cen = vg - mu
    var = jnp.mean(cen * cen, axis=-1, keepdims=True)
    vn_ref[...] = (cen * lax.rsqrt(var + NORM_EPS) * gnw_ref[...]).astype(BF16)


def _od_inproj(x3d, nw, w_in, gnw, cos, sin):
    b, s, _ = x3d.shape
    tm = ROW_TILE
    row = lambda bi, i: (bi, i, 0)
    const = lambda bi, i: (0, 0)
    r4, r16 = DILATIONS[1], DILATIONS[2]
    width = 3 * ATT_WIDTH
    return pl.pallas_call(
        _od_inproj_kernel,
        grid=(b, s // tm),
        in_specs=[
            pl.BlockSpec((None, tm, D_MODEL), row),
            pl.BlockSpec((1, D_MODEL), const),
            pl.BlockSpec(w_in.shape, const),
            pl.BlockSpec((1, GMLP_WIDTH), const),
            pl.BlockSpec((tm, LANES), lambda bi, i: (i, 0)),
            pl.BlockSpec((tm, LANES), lambda bi, i: (i, 0)),
        ],
        out_specs=[
            pl.BlockSpec((None, tm, width), row),
            pl.BlockSpec((None, r4, tm // r4, width), lambda bi, i: (bi, 0, i, 0)),
            pl.BlockSpec((None, r16, tm // r16, width), lambda bi, i: (bi, 0, i, 0)),
            pl.BlockSpec((None, tm, GMLP_WIDTH), row),
            pl.BlockSpec((None, tm, GMLP_WIDTH), row),
        ],
        out_shape=[jax.ShapeDtypeStruct((b, s, width), BF16),
                   jax.ShapeDtypeStruct((b, r4, s // r4, width), BF16),
                   jax.ShapeDtypeStruct((b, r16, s // r16, width), BF16),
                   jax.ShapeDtypeStruct((b, s, GMLP_WIDTH), BF16), jax.ShapeDtypeStruct((b, s, GMLP_WIDTH), BF16)],
        scratch_shapes=[pltpu.VMEM((QKV_SLABS, tm, LANES), F32), pltpu.VMEM((QKV_SLABS, tm, LANES), F32)],
        compiler_params=_params("parallel", "parallel"),
        name="od_inproj",
    )(x3d, nw, w_in, gnw, cos, sin)


ATT_Q_TILE = 128


def _attn_kernel(q_ref, kp_ref, kc_ref, kn_ref, vp_ref, vc_ref, vn_ref, o_ref, lse_ref, *, dilation):
    i = pl.program_id(1)
    last = pl.num_programs(1) - 1
    rows = pl.ds(pl.program_id(2), ATT_Q_TILE, stride=dilation) if dilation > 1 else slice(None)
    nk = ATT_Q_TILE + 2 * ATT_HALF
    qq = lax.broadcasted_iota(jnp.int32, (ATT_Q_TILE, nk), 0)
    kk = lax.broadcasted_iota(jnp.int32, (ATT_Q_TILE, nk), 1)
    k_lo = jnp.where(i > 0, 0, ATT_HALF)
    k_hi = jnp.where(i < last, nk - 1, ATT_HALF + ATT_Q_TILE - 1)
    valid = (kk >= jnp.maximum(qq, k_lo)) & (kk <= jnp.minimum(qq + 2 * ATT_HALF, k_hi))
    lane = lax.broadcasted_iota(jnp.int32, (ATT_Q_TILE, LANES), 1)
    first = lane < ATT_HEAD_DIM
    masks = (jnp.where(first[0:1, :], 1.0, 0.0).astype(BF16), jnp.where(first[0:1, :], 0.0, 1.0).astype(BF16))
    lse_all = jnp.zeros((ATT_Q_TILE, LANES), F32)
    for pr in range(ATT_HEADS // 2):
        cols = slice(pr * LANES, (pr + 1) * LANES)
        q2 = q_ref[:, cols]
        k2 = jnp.concatenate([kp_ref[:, cols], kc_ref[:, cols], kn_ref[:, cols]], axis=0)
        v2 = jnp.concatenate([vp_ref[:, cols], vc_ref[:, cols], vn_ref[:, cols]], axis=0)
        outs = []
        for half in range(2):
            s = lax.dot_general(q2 * masks[half], k2, (((1,), (1,)), ((), ())), preferred_element_type=F32)
            s = jnp.where(valid, s, NEG_BIG)
            m = jnp.max(s, axis=-1, keepdims=True)
            p = jnp.exp2(s - m)
            den = jnp.sum(p, axis=-1, keepdims=True)
            outs.append(jnp.dot(p.astype(BF16), v2, preferred_element_type=F32) * (1.0 / den))
            lse_all = jnp.where(lane == 2 * pr + half, m * LN2 + jnp.log(den), lse_all)
        o_ref[pr, rows, :] = jnp.where(first, outs[0], outs[1])
    lse_ref[rows, :] = lse_all


def _attn(qkv, seq):
    b, r, l, _ = qkv.shape
    tq = ATT_Q_TILE
    per = tq // ATT_HALF
    nhalf = l // ATT_HALF
    pairs = ATT_HEADS // 2
    main = lambda col: pl.BlockSpec((None, None, tq, ATT_WIDTH), lambda bi, i, c: (bi, c, i, col))
    prev = lambda col: pl.BlockSpec((None, None, ATT_HALF, ATT_WIDTH),
                                    lambda bi, i, c: (bi, c, jnp.maximum(i * per - 1, 0), col))
    nxt = lambda col: pl.BlockSpec((None, None, ATT_HALF, ATT_WIDTH),
                                   lambda bi, i, c: (bi, c, jnp.minimum((i + 1) * per, nhalf - 1), col))
    return pl.pallas_call(
        functools.partial(_attn_kernel, dilation=r),
        grid=(b, l // tq, r),
        in_specs=[main(0), prev(1), main(1), nxt(1), prev(2), main(2), nxt(2)],
        out_specs=[
            pl.BlockSpec((None, pairs, tq * r, LANES), lambda bi, i, c: (bi, 0, i, 0)),
            pl.BlockSpec((None, tq * r, LANES), lambda bi, i, c: (bi, i, 0)),
        ],
        out_shape=[jax.ShapeDtypeStruct((b, pairs, seq, LANES), F32), jax.ShapeDtypeStruct((b, seq, LANES), F32)],
        compiler_params=_params("parallel", "parallel", "arbitrary"),
        name=f"od_attn_r{r}",
    )(qkv, qkv, qkv, qkv, qkv, qkv, qkv)


def _od_out_kernel(o1_ref, o2_ref, o3_ref, l1_ref, l2_ref, l3_ref, u_ref, vn_ref, ws_ref, bs_ref, x_ref, w_ref,
                   npost_ref, ex_ref, out_ref):
    tm = x_ref.shape[0]
    l1, l2, l3 = l1_ref[...], l2_ref[...], l3_ref[...]
    m = jnp.maximum(jnp.maximum(l1, l2), l3)
    e1, e2, e3 = jnp.exp(l1 - m), jnp.exp(l2 - m), jnp.exp(l3 - m)
    inv = 1.0 / (e1 + e2 + e3)

    def per_head_lanes(w):
        hi = w.astype(BF16)
        lo = (w - hi.astype(F32)).astype(BF16)
        return (jnp.dot(hi, ex_ref[...], preferred_element_type=F32) + jnp.dot(lo, ex_ref[...], preferred_element_type=F32))

    def token_major(o_ref):
        return jnp.concatenate([o_ref[pr] for pr in range(ATT_HEADS // 2)], axis=1)

    o3 = token_major(o3_ref)
    att = (o3 + per_head_lanes(e1 * inv) * (token_major(o1_ref) - o3)
           + per_head_lanes(e2 * inv) * (token_major(o2_ref) - o3))
    first = lax.broadcasted_iota(jnp.int32, (CHUNK, LANES), 1) < GMLP_GROUP_DIM
    gate_rows = []
    for c in range(tm // CHUNK):
        rows = slice(c * CHUNK, (c + 1) * CHUNK)
        parts = []
        for pr in range(GMLP_GROUPS // 2):
            cols = slice(pr * LANES, (pr + 1) * LANES)
            vn2 = vn_ref[rows, cols]
            mixed = jnp.where(first, jnp.dot(ws_ref[2 * pr], vn2, preferred_element_type=F32),
                              jnp.dot(ws_ref[2 * pr + 1], vn2, preferred_element_type=F32)) + bs_ref[:, cols]
            parts.append((u_ref[rows, cols].astype(F32) * mixed).astype(BF16))
        gate_rows.append(jnp.concatenate(parts, axis=1))
    cat = jnp.concatenate([att.astype(BF16), jnp.concatenate(gate_rows, axis=0)], axis=1)
    mix = jnp.dot(cat, w_ref[...], preferred_element_type=F32)
    out_ref[...] = x_ref[...] + _rms(mix, npost_ref[...])


def _od_out(o1, o2, o3, l1, l2, l3, u, vn, ws, bs, x3d, w_out, npost):
    b, s, _ = x3d.shape
    tm = ROW_TILE
    row = lambda bi, i: (bi, i, 0)
    const = lambda bi, i: (0, 0)
    slabs = pl.BlockSpec((None, ATT_HEADS // 2, tm, LANES), lambda bi, i: (bi, 0, i, 0))
    head_of_lane = jnp.arange(ATT_WIDTH) // ATT_HEAD_DIM
    expand = (jnp.arange(LANES)[:, None] == head_of_lane[None, :]).astype(BF16)
    return pl.pallas_call(
        _od_out_kernel,
        grid=(b, s // tm),
        in_specs=[
            slabs, slabs, slabs,
            pl.BlockSpec((None, tm, LANES), row), pl.BlockSpec((None, tm, LANES), row), pl.BlockSpec((None, tm, LANES), row),
            pl.BlockSpec((None, tm, GMLP_WIDTH), row), pl.BlockSpec((None, tm, GMLP_WIDTH), row),
            pl.BlockSpec((GMLP_GROUPS, CHUNK, CHUNK), lambda bi, i: (0, 0, 0)),
            pl.BlockSpec((CHUNK, GMLP_WIDTH), const),
            pl.BlockSpec((None, tm, D_MODEL), row),
            pl.BlockSpec((OD_MIX, D_MODEL), const),
            pl.BlockSpec((1, D_MODEL), const),
            pl.BlockSpec((LANES, ATT_WIDTH), const),
        ],
        out_specs=pl.BlockSpec((None, tm, D_MODEL), row),
        out_shape=jax.ShapeDtypeStruct((b, s, D_MODEL), F32),
        compiler_params=_params("parallel", "parallel"),
        name="od_out",
    )(o1, o2, o3, l1, l2, l3, u, vn, ws, bs, x3d, w_out, npost, expand)


def _rope_tables(seq):
    inv_freq = ROPE_THETA ** (-jnp.arange(0, ATT_HEAD_DIM, 2, dtype=F32) / ATT_HEAD_DIM)
    ang = jnp.arange(seq, dtype=F32)[:, None] * inv_freq[None, :]
    ang = jnp.concatenate([ang, ang, ang, ang], axis=-1)
    sign = jnp.where((jnp.arange(LANES) % 64) < 32, -1.0, 1.0).astype(F32)
    return jnp.cos(ang), jnp.sin(ang) * sign[None, :]


def _dt_lanes(per_dir):
    fwd = jnp.tile(per_dir[0], 3)
    bwd = jnp.tile(per_dir[1], 3)
    pad = jnp.zeros((64 - 3 * SSD_HEADS,), F32)
    return jnp.concatenate([fwd, pad, bwd, pad])[None, :]


def _prep_even(ev_in_proj, ev_conv_w, ev_conv_b, ssd_dt_bias, ssd_a_log, ssd_d, ssd_norm_w, ret_decay, ret_gn_w, ev_out_proj):
    wz, wxbc, wdt, wq, wk, wv, wg = jnp.split(ev_in_proj, [1024, 2560, 2592, 3104, 3616, 4640], axis=-1)
    w_main = jnp.concatenate([wz, wxbc, wq, wk, wv, wg], axis=-1).astype(BF16)
    wdt_f, wdt_b = wdt[:, :SSD_HEADS], wdt[:, SSD_HEADS:]
    zpad = jnp.zeros((D_MODEL, 64 - 3 * SSD_HEADS), F32)
    w_dt = jnp.concatenate([wdt_f, wdt_f, wdt_f, zpad, wdt_b, wdt_b, wdt_b, zpad], axis=-1).astype(BF16)
    conv_w = jnp.concatenate([ev_conv_w, jnp.zeros((8 - SSD_CONV, SSD_CONV_CH), F32)], axis=0)
    return dict(
        w_main=w_main, w_dt=w_dt, dt_bias=_dt_lanes(ssd_dt_bias), a_log=_dt_lanes(ssd_a_log),
        conv_w=conv_w, conv_b=ev_conv_b[None, :],
        dskip=jnp.repeat(ssd_d, SSD_HEAD_DIM)[None, :], nssd=ssd_norm_w[None, :], ngn=ret_gn_w[None, :],
        rd=jnp.broadcast_to(ret_decay[:, :, None], (2, RET_HEADS, LANES)),
        w_out=ev_out_proj.astype(BF16),
    )


def _even_layer(x2d, b, s, p, nmix_pre, nmix_post, cos, sin):
    z, xbc, q, k, v, g, dt, dtt = _ev_inproj(x2d, s, nmix_pre, p["w_main"], p["w_dt"], p["dt_bias"], p["a_log"], cos, sin)
    xbc_act = _conv(xbc.reshape(b, s, SSD_CONV_CH), p["conv_w"], p["conv_b"])
    y = _ssd_ret(xbc_act, dt.reshape(2, b, s, LANES), dtt.reshape(2, b, s, LANES), q.reshape(b, s, -1),
                 k.reshape(b, s, -1), v.reshape(b, s, -1), p["rd"])
    return _ev_out(y.reshape(2, b * s, EV_MIX), xbc_act.reshape(b * s, SSD_CONV_CH), z, g, x2d, p["dskip"], p["nssd"],
                   p["ngn"], p["w_out"], nmix_post)


def _odd_layer(x2d, b, s, p, nmix_pre, nmix_post, cos, sin):
    x3d = x2d.reshape(b, s, D_MODEL)
    qkv1, qkv4, qkv16, u, vn = _od_inproj(x3d, nmix_pre, p["w_in"], p["gnw"], cos, sin)
    (o1, l1), (o2, l2), (o3, l3) = [_attn(qkv, s) for qkv in (qkv1[:, None], qkv4, qkv16)]
    out = _od_out(o1, o2, o3, l1, l2, l3, u, vn, p["ws"], p["bs"], x3d, p["w_out"], nmix_post)
    return out.reshape(b * s, D_MODEL)


def _trunk(x, even, odd, norm_mix_pre, norm_mix_post, norm_ffn_pre, norm_ffn_post, w1, w2):
    b, s, _ = x.shape
    cos, sin = _rope_tables(s)
    x2d = x.reshape(b * s, D_MODEL)
    depth = norm_mix_pre.shape[0]
    for i in range(depth):
        j = i // 2
        layer = _even_layer if i % 2 == 0 else _odd_layer
        params = even[j] if i % 2 == 0 else odd[j]
        x2d = layer(x2d, b, s, params, norm_mix_pre[i][None, :], norm_mix_post[i][None, :], cos, sin)
        x2d = _ffn(x2d, norm_ffn_pre[i][None, :], w1[i], w2[i], norm_ffn_post[i][None, :])
    return x2d.reshape(b, s, D_MODEL)


def kernel(x_prompt, x_sample, norm_mix_pre, norm_mix_post, norm_ffn_pre, norm_ffn_post, ffn_w1, ffn_w2, ev_in_proj, ev_conv_w, ev_conv_b, ssd_dt_bias, ssd_a_log, ssd_d, ssd_norm_w, ret_decay, ret_gn_w, ev_out_proj, od_in_proj, gmlp_norm_w, gmlp_ws, gmlp_bs, od_out_proj):
    even = [_prep_even(ev_in_proj[j], ev_conv_w[j], ev_conv_b[j], ssd_dt_bias[j], ssd_a_log[j], ssd_d[j], ssd_norm_w[j],
                       ret_decay[j], ret_gn_w[j], ev_out_proj[j]) for j in range(ev_in_proj.shape[0])]
    odd = [dict(w_in=od_in_proj[j].astype(BF16), gnw=gmlp_norm_w[j][None, :], ws=gmlp_ws[j].astype(BF16),
                bs=jnp.repeat(gmlp_bs[j].T, GMLP_GROUP_DIM, axis=1), w_out=od_out_proj[j].astype(BF16))
           for j in range(od_in_proj.shape[0])]
    w1 = ffn_w1.astype(BF16)
    w2 = ffn_w2.astype(BF16)
    run = functools.partial(_trunk, even=even, odd=odd, norm_mix_pre=norm_mix_pre, norm_mix_post=norm_mix_post,
                            norm_ffn_pre=norm_ffn_pre, norm_ffn_post=norm_ffn_post, w1=w1, w2=w2)
    return (run(x_prompt), run(x_sample))
```

```python
import functools
import math

import jax
import jax.numpy as jnp
from jax import lax
from jax.experimental import pallas as pl
from jax.experimental.pallas import tpu as pltpu

F32 = jnp.float32
BF16 = jnp.bfloat16

D_MODEL = 1024
NORM_EPS = 1e-6
ROPE_THETA = 10000.0
CHUNK = 128
LANES = 128

SSD_HEADS = 16
SSD_HEAD_DIM = 64
SSD_WIDTH = SSD_HEADS * SSD_HEAD_DIM
SSD_GROUPS = 2
SSD_HPG = SSD_HEADS // SSD_GROUPS
SSD_STATE = 128
SSD_CONV = 5
SSD_BC = 2 * SSD_GROUPS * SSD_STATE
SSD_CONV_CH = SSD_WIDTH + SSD_BC

RET_HEADS = 8
RET_QK_DIM = 64
RET_V_DIM = 128
RET_QK_WIDTH = RET_HEADS * RET_QK_DIM
RET_V_WIDTH = RET_HEADS * RET_V_DIM
EV_MIX = SSD_WIDTH + RET_V_WIDTH

ATT_HEADS = 16
ATT_HEAD_DIM = 64
ATT_WIDTH = ATT_HEADS * ATT_HEAD_DIM
DILATED_PATTERNS = ((128, 1), (512, 4), (2048, 16))
ATT_HALF = 64

GMLP_GROUPS = 8
GMLP_GROUP_DIM = 64
GMLP_WIDTH = GMLP_GROUPS * GMLP_GROUP_DIM
OD_MIX = ATT_WIDTH + GMLP_WIDTH
FFN_HIDDEN = 4 * D_MODEL

EV_Z0, EV_XBC0, EV_Q0, EV_K0, EV_V0, EV_G0, EV_END = 0, 1024, 2560, 3072, 3584, 4608, 5632
DT_LANE, CS_LANE, TOT_LANE = 0, 16, 32

ROW_TILE = 256
FFN_ROW_TILE = 512
SUB_ROWS = 256
CONV_HALO = 16
CONV_COLS = 256
VMEM_LIMIT = 56 * 1024 * 1024
NEG_BIG = -1e30
LOG2E = 1.4426950408889634
LN2 = 0.6931471805599453


def _params(*sem):
    return pltpu.CompilerParams(dimension_semantics=sem, vmem_limit_bytes=VMEM_LIMIT)


def _rms(x, w):
    return x * lax.rsqrt(jnp.mean(x * x, axis=-1, keepdims=True) + NORM_EPS) * w


def _silu(x):
    return x / (1.0 + jnp.exp(-x))


def _softplus(x):
    return jnp.maximum(x, 0.0) + jnp.log1p(jnp.exp(-jnp.abs(x)))


def _rope(t, cos, sin_signed, first_half):
    rot = jnp.where(first_half, pltpu.roll(t, 96, 1), pltpu.roll(t, 32, 1))
    return t * cos + rot * sin_signed


def _rope_cols(a, cos, sin_signed):
    lane = lax.broadcasted_iota(jnp.int32, (a.shape[0], LANES), 1)
    first_half = (lane % 64) < 32
    return [_rope(a[:, j * LANES:(j + 1) * LANES], cos, sin_signed, first_half) for j in range(a.shape[1] // LANES)]


def _split3(x):
    hi = x.astype(BF16)
    r1 = x - hi.astype(F32)
    mid = r1.astype(BF16)
    lo = (r1 - mid.astype(F32)).astype(BF16)
    return hi, mid, lo


def _ev_inproj_kernel(x_ref, xp_ref, xn_ref, nw_ref, w_ref, wdt_ref, dtb_ref, alog_ref, cw_ref, cb_ref, cos_ref, sin_ref,
                      z_ref, xbc_ref, q_ref, k_ref, v_ref, g_ref, dt_ref, dtt_ref, ext_ref):
    i = pl.program_id(1)
    last = pl.num_programs(1) - 1
    tm = x_ref.shape[0]
    nw = nw_ref[...]
    hn = _rms(x_ref[...], nw).astype(BF16)

    def mm(c0, c1):
        return jnp.dot(hn, w_ref[:, c0:c1], preferred_element_type=F32)

    cos, sin = cos_ref[...], sin_ref[...]

    def seg_z():
        z_ref[...] = mm(EV_Z0, EV_XBC0).astype(BF16)

    def seg_q():
        for j, t in enumerate(_rope_cols(mm(EV_Q0, EV_K0), cos, sin)):
            q_ref[:, j * LANES:(j + 1) * LANES] = t.astype(BF16)

    def seg_k():
        for j, t in enumerate(_rope_cols(mm(EV_K0, EV_V0), cos, sin)):
            k_ref[:, j * LANES:(j + 1) * LANES] = (t * (RET_QK_DIM ** -0.5)).astype(BF16)

    def seg_v():
        v_ref[...] = mm(EV_V0, EV_G0).astype(BF16)

    def seg_g():
        g_ref[...] = mm(EV_G0, EV_END).astype(BF16)

    hn_ext = jnp.concatenate([_rms(xp_ref[...], nw).astype(BF16), hn, _rms(xn_ref[...], nw).astype(BF16)], axis=0)
    erow = lax.broadcasted_iota(jnp.int32, (tm + 2 * CONV_HALO, 1), 0)
    lo = jnp.where(i > 0, 0, CONV_HALO)
    hi = jnp.where(i < last, tm + 2 * CONV_HALO, tm + CONV_HALO)
    inside = (erow >= lo) & (erow < hi)
    others = [seg_z, seg_q, seg_k, seg_v, seg_g]
    for blk, c0 in enumerate(range(0, SSD_CONV_CH, CONV_COLS)):
        cols = slice(c0, c0 + CONV_COLS)
        ext = jnp.dot(hn_ext, w_ref[:, EV_XBC0 + c0:EV_XBC0 + c0 + CONV_COLS], preferred_element_type=F32)
        ext_ref[blk % 2] = jnp.where(inside, ext, 0.0)
        acc = jnp.broadcast_to(cb_ref[:, cols], (tm, CONV_COLS))
        for j in range(SSD_CONV):
            off = CONV_HALO - SSD_CONV // 2 + j
            acc = acc + cw_ref[j:j + 1, cols] * ext_ref[blk % 2, off:off + tm, :]
        xbc_ref[:, cols] = _silu(acc).astype(BF16)
        if blk < len(others):
            others[blk]()

    raw = jnp.dot(hn, wdt_ref[...], preferred_element_type=F32)
    dt = _softplus(raw + dtb_ref[...])
    la = dt * (-jnp.exp(alog_ref[...]))
    lane = lax.broadcasted_iota(jnp.int32, (CHUNK, LANES), 1)
    sub = lane % 64
    is_dt = sub < CS_LANE
    is_cs = (sub >= CS_LANE) & (sub < TOT_LANE)
    is_bwd = lane >= 64
    row = lax.broadcasted_iota(jnp.int32, (CHUNK, CHUNK), 0)
    col = lax.broadcasted_iota(jnp.int32, (CHUNK, CHUNK), 1)
    tri = jnp.where(row >= col, 1.0, 0.0).astype(BF16)
    for c in range(tm // CHUNK):
        rows = slice(c * CHUNK, (c + 1) * CHUNK)
        la_c = la[rows]
        cs = sum(jnp.dot(tri, part, preferred_element_type=F32) for part in _split3(la_c))
        tot = jnp.broadcast_to(cs[CHUNK - 1:CHUNK, :], cs.shape)
        rcs = tot - cs + la_c
        rec = jnp.where(is_dt, dt[rows], jnp.where(is_cs, jnp.where(is_bwd, rcs, cs), tot))
        rec_b = pltpu.roll(rec, 64, 1)
        dt_ref[0, rows, :] = rec
        dt_ref[1, rows, :] = rec_b
        dtt_ref[0, rows, :] = rec.T
        dtt_ref[1, rows, :] = rec_b.T


def _ev_inproj(x3d, nw, w_main, w_dt, dt_bias, a_log, conv_w, conv_b, cos, sin):
    b, s, _ = x3d.shape
    tm = ROW_TILE
    per = tm // CONV_HALO
    nh = s // CONV_HALO
    row = lambda bi, i: (bi, i, 0)
    const = lambda bi, i: (0, 0)
    bf = lambda n: jax.ShapeDtypeStruct((b, s, n), BF16)
    rec = pl.BlockSpec((2, None, tm, LANES), lambda bi, i: (0, bi, i, 0))
    return pl.pallas_call(
        _ev_inproj_kernel,
        grid=(b, s // tm),
        in_specs=[
            pl.BlockSpec((None, tm, D_MODEL), row),
            pl.BlockSpec((None, CONV_HALO, D_MODEL), lambda bi, i: (bi, jnp.maximum(i * per - 1, 0), 0)),
            pl.BlockSpec((None, CONV_HALO, D_MODEL), lambda bi, i: (bi, jnp.minimum((i + 1) * per, nh - 1), 0)),
            pl.BlockSpec((1, D_MODEL), const),
            pl.BlockSpec(w_main.shape, const),
            pl.BlockSpec(w_dt.shape, const),
            pl.BlockSpec((1, LANES), const),
            pl.BlockSpec((1, LANES), const),
            pl.BlockSpec(conv_w.shape, const),
            pl.BlockSpec((1, SSD_CONV_CH), const),
            pl.BlockSpec((tm, LANES), lambda bi, i: (i, 0)),
            pl.BlockSpec((tm, LANES), lambda bi, i: (i, 0)),
        ],
        out_specs=[
            pl.BlockSpec((None, tm, SSD_WIDTH), row),
            pl.BlockSpec((None, tm, SSD_CONV_CH), row),
            pl.BlockSpec((None, tm, RET_QK_WIDTH), row),
            pl.BlockSpec((None, tm, RET_QK_WIDTH), row),
            pl.BlockSpec((None, tm, RET_V_WIDTH), row),
            pl.BlockSpec((None, tm, RET_V_WIDTH), row),
            rec, rec,
        ],
        out_shape=[bf(SSD_WIDTH), bf(SSD_CONV_CH), bf(RET_QK_WIDTH), bf(RET_QK_WIDTH), bf(RET_V_WIDTH), bf(RET_V_WIDTH),
                   jax.ShapeDtypeStruct((2, b, s, LANES), F32), jax.ShapeDtypeStruct((2, b, s, LANES), F32)],
        scratch_shapes=[pltpu.VMEM((2, tm + 2 * CONV_HALO, CONV_COLS), F32)],
        compiler_params=_params("parallel", "parallel"),
        name="ev_inproj",
    )(x3d, x3d, x3d, nw, w_main, w_dt, dt_bias, a_log, conv_w, conv_b, cos, sin)


def _ssd_ret_kernel(xs_ref, bc_ref, dt_ref, dtt_ref, q_ref, k_ref, v_ref, rd_ref, y_ref, h_ref, r_ref, dm_ref, dec_ref):
    d = pl.program_id(1)
    step = pl.program_id(2)
    fwd = d == 0
    row = lax.broadcasted_iota(jnp.int32, (CHUNK, CHUNK), 0)
    col = lax.broadcasted_iota(jnp.int32, (CHUNK, CHUNK), 1)
    ahead = jnp.where(fwd, row - col, col - row)
    causal = ahead >= 0
    first = col < SSD_HEAD_DIM
    first_row = first[0:1, :]
    m_lo = jnp.where(first_row, 1.0, 0.0).astype(BF16)
    m_hi = jnp.where(first_row, 0.0, 1.0).astype(BF16)

    @pl.when(step == 0)
    def _():
        h_ref[...] = jnp.zeros_like(h_ref)
        r_ref[...] = jnp.zeros_like(r_ref)
        rd = rd_ref[...]
        lg = jnp.minimum(rd, 0.0) - jnp.log1p(jnp.exp(-jnp.abs(rd)))
        rowf = row.astype(F32)
        dist = ahead.astype(F32)
        posq = jnp.where(fwd, rowf + 1.0, CHUNK - rowf)
        posk = jnp.where(fwd, CHUNK - 1.0 - rowf, rowf)
        for h in range(RET_HEADS):
            dm_ref[h] = jnp.where(causal, jnp.exp(lg[h:h + 1, :] * dist), 0.0)
        for pr in range(RET_HEADS // 2):
            lg_a, lg_b = lg[2 * pr:2 * pr + 1, :], lg[2 * pr + 1:2 * pr + 2, :]
            lg2 = jnp.where(first_row, lg_a, lg_b)
            dec_ref[0, pr] = jnp.exp(lg2 * posq)
            dec_ref[1, pr] = jnp.exp(lg2 * posk)
            dec_ref[2, pr] = jnp.exp(jnp.where(row < RET_QK_DIM, lg_a, lg_b) * float(CHUNK))

    nt = (((1,), (1,)), ((), ()))

    def ret_pair(pr):
        cols = slice(pr * LANES, (pr + 1) * LANES)
        q2 = q_ref[:, cols]
        k2 = k_ref[:, cols]
        rp = r_ref[pr]
        rpb = rp.astype(BF16)
        qd2 = (q2.astype(F32) * dec_ref[0, pr]).astype(BF16)
        kdt = (k2.astype(F32) * dec_ref[1, pr]).T.astype(BF16)
        st = []
        for half, msk in enumerate((m_lo, m_hi)):
            h = 2 * pr + half
            vh = v_ref[:, h * RET_V_DIM:(h + 1) * RET_V_DIM]
            sc = lax.dot_general(q2 * msk, k2, nt, preferred_element_type=F32) * dm_ref[h]
            y = (jnp.dot(sc.astype(BF16), vh, preferred_element_type=F32)
                 + jnp.dot(qd2 * msk, rpb, preferred_element_type=F32))
            y_ref[:, SSD_WIDTH + h * RET_V_DIM:SSD_WIDTH + (h + 1) * RET_V_DIM] = y.astype(BF16)
            st.append(jnp.dot(kdt[half * RET_QK_DIM:(half + 1) * RET_QK_DIM, :], vh, preferred_element_type=F32))
        r_ref[pr] = rp * dec_ref[2, pr] + jnp.concatenate(st, axis=0)

    rec2 = dt_ref[...] * LOG2E
    rect = dtt_ref[...]
    rect2 = rect * LOG2E

    def ssd_pair(g, pr, cb, hg, yoff, bgt):
        cols = slice((g * SSD_HPG // 2 + pr) * LANES, (g * SSD_HPG // 2 + pr + 1) * LANES)
        lanes = slice(pr * LANES, (pr + 1) * LANES)
        x2 = xs_ref[:, cols]
        yd, csb, st, dec = [], [], [], []
        for half in range(2):
            e = g * SSD_HPG + 2 * pr + half
            cs_row = rect2[CS_LANE + e:CS_LANE + e + 1, :]
            dt_row = rect[DT_LANE + e:DT_LANE + e + 1, :]
            tot_row = rect2[TOT_LANE + e:TOT_LANE + e + 1, :]
            csb.append(jnp.broadcast_to(rec2[:, CS_LANE + e:CS_LANE + e + 1], (CHUNK, CHUNK)))
            decay = jnp.where(causal, jnp.exp2(csb[half] - cs_row), 0.0)
            me = (cb * decay * dt_row).astype(BF16)
            yd.append(jnp.dot(me, x2, preferred_element_type=F32))
            bw = (bgt * (dt_row * jnp.exp2(tot_row - cs_row))).astype(BF16)
            st.append(jnp.dot(bw, x2, preferred_element_type=F32))
            dec.append(jnp.exp2(tot_row))
        y = jnp.where(first, yd[0], yd[1]) + yoff[:, lanes] * jnp.exp2(jnp.where(first, csb[0], csb[1]))
        y_ref[:, cols] = y.astype(BF16)
        h_ref[g, :, lanes] = hg[:, lanes] * jnp.where(first_row, dec[0], dec[1]) + jnp.where(first, st[0], st[1])

    for g in range(SSD_GROUPS):
        bg = bc_ref[:, g * SSD_STATE:(g + 1) * SSD_STATE]
        cg = bc_ref[:, (SSD_GROUPS + g) * SSD_STATE:(SSD_GROUPS + g + 1) * SSD_STATE]
        cb = lax.dot_general(cg, bg, nt, preferred_element_type=F32)
        hg = h_ref[g]
        yoff = jnp.dot(cg, hg.astype(BF16), preferred_element_type=F32)
        bgt = bg.astype(F32).T
        for pr in range(SSD_HPG // 2):
            if pr % 2 == 0:
                ret_pair(g * 2 + pr // 2)
            ssd_pair(g, pr, cb, hg, yoff, bgt)


def _ssd_ret(xbc_act, dt, dtt, q, k, v, rd):
    b, s, _ = xbc_act.shape
    nc = s // CHUNK

    def chunk(d, i):
        return i + d * (nc - 1 - 2 * i)

    seq = lambda bi, d, i: (bi, chunk(d, i), 0)
    rec = lambda bi, d, i: (d, bi, chunk(d, i), 0)
    return pl.pallas_call(
        _ssd_ret_kernel,
        grid=(b, 2, nc),
        in_specs=[
            pl.BlockSpec((None, CHUNK, SSD_WIDTH), seq),
            pl.BlockSpec((None, CHUNK, SSD_BC), lambda bi, d, i: (bi, chunk(d, i), SSD_WIDTH // SSD_BC)),
            pl.BlockSpec((None, None, CHUNK, LANES), rec),
            pl.BlockSpec((None, None, CHUNK, LANES), rec),
            pl.BlockSpec((None, CHUNK, RET_QK_WIDTH), seq),
            pl.BlockSpec((None, CHUNK, RET_QK_WIDTH), seq),
            pl.BlockSpec((None, CHUNK, RET_V_WIDTH), seq),
            pl.BlockSpec((None, RET_HEADS, LANES), lambda bi, d, i: (d, 0, 0)),
        ],
        out_specs=pl.BlockSpec((None, None, CHUNK, EV_MIX), rec),
        out_shape=jax.ShapeDtypeStruct((2, b, s, EV_MIX), BF16),
        scratch_shapes=[pltpu.VMEM((SSD_GROUPS, SSD_STATE, SSD_HPG * SSD_HEAD_DIM), F32),
                        pltpu.VMEM((RET_HEADS // 2, 2 * RET_QK_DIM, RET_V_DIM), F32),
                        pltpu.VMEM((RET_HEADS, CHUNK, CHUNK), F32),
                        pltpu.VMEM((3, RET_HEADS // 2, CHUNK, CHUNK), F32)],
        compiler_params=_params("parallel", "arbitrary", "arbitrary"),
        name="ev_ssd_ret",
    )(xbc_act, xbc_act, dt, dtt, q, k, v, rd)


FFN_HIDDEN_TILE = 1024


def _ffn_chunk(hn, w1_ref, w2_ref, c):
    cols = slice(c * FFN_HIDDEN_TILE, (c + 1) * FFN_HIDDEN_TILE)
    a = jnp.maximum(jnp.dot(hn, w1_ref[:, cols], preferred_element_type=F32), 0.0)
    return jnp.dot((a * a).astype(BF16), w2_ref[cols, :], preferred_element_type=F32)


def _mix_then_ffn(mix_rows, o_ref, nfpre_ref, w1_ref, w2_ref, nfpost_ref):
    n = o_ref.shape[0]
    halves = [slice(r0, r0 + n // 2) for r0 in (0, n // 2)]
    chunks = range(FFN_HIDDEN // FFN_HIDDEN_TILE)
    xa = mix_rows(halves[0])
    hna = _rms(xa, nfpre_ref[...]).astype(BF16)
    acc = _ffn_chunk(hna, w1_ref, w2_ref, 0)
    xb = mix_rows(halves[1])
    for c in chunks[1:]:
        acc = acc + _ffn_chunk(hna, w1_ref, w2_ref, c)
    o_ref[halves[0], :] = xa + _rms(acc, nfpost_ref[...])
    hnb = _rms(xb, nfpre_ref[...]).astype(BF16)
    acc = _ffn_chunk(hnb, w1_ref, w2_ref, 0)
    for c in chunks[1:]:
        acc = acc + _ffn_chunk(hnb, w1_ref, w2_ref, c)
    o_ref[halves[1], :] = xb + _rms(acc, nfpost_ref[...])


def _resident(shape):
    return pl.BlockSpec(shape, lambda *_: (0,) * len(shape), pipeline_mode=pl.Buffered(1))


def _ev_out_kernel(yf_ref, yb_ref, xs_ref, z_ref, g_ref, x_ref, dskip_ref, nssd_ref, ngn_ref, w_ref, npost_ref,
                   nfpre_ref, w1_ref, w2_ref, nfpost_ref, o_ref):
    gw = SSD_WIDTH // SSD_GROUPS

    def mix_rows(rows):
        yf = yf_ref[rows, :].astype(F32)
        yb = yb_ref[rows, :].astype(F32)
        y = yf[:, :SSD_WIDTH] + yb[:, :SSD_WIDTH] + xs_ref[rows, :].astype(F32) * dskip_ref[...]
        gg = y * _silu(z_ref[rows, :].astype(F32))
        parts = []
        for g in range(SSD_GROUPS):
            part = gg[:, g * gw:(g + 1) * gw]
            part = (part * lax.rsqrt(jnp.mean(part * part, axis=-1, keepdims=True) + NORM_EPS)
                    * nssd_ref[:, g * gw:(g + 1) * gw])
            parts.append(part.astype(BF16))
        r = yf[:, SSD_WIDTH:] + yb[:, SSD_WIDTH:]
        gate = _silu(g_ref[rows, :].astype(F32))
        for h in range(RET_HEADS):
            lanes = slice(h * RET_V_DIM, (h + 1) * RET_V_DIM)
            part = r[:, lanes]
            mu = jnp.mean(part, axis=-1, keepdims=True)
            cen = part - mu
            var = jnp.mean(cen * cen, axis=-1, keepdims=True)
            parts.append((cen * lax.rsqrt(var + NORM_EPS) * ngn_ref[:, lanes] * gate[:, lanes]).astype(BF16))
        mix = jnp.dot(jnp.concatenate(parts, axis=1), w_ref[...], preferred_element_type=F32)
        return x_ref[rows, :] + _rms(mix, npost_ref[...])

    _mix_then_ffn(mix_rows, o_ref, nfpre_ref, w1_ref, w2_ref, nfpost_ref)


def _ev_out(y, xbc_act2d, z, g, x2d, dskip, nssd, ngn, w_out, npost, nfpre, w1, w2, nfpost):
    t = x2d.shape[0]
    tm = FFN_ROW_TILE
    row = lambda i: (i, 0)
    vec = lambda n: _resident((1, n))
    return pl.pallas_call(
        _ev_out_kernel,
        grid=(t // tm,),
        in_specs=[
            pl.BlockSpec((None, tm, EV_MIX), lambda i: (0, i, 0)),
            pl.BlockSpec((None, tm, EV_MIX), lambda i: (1, i, 0)),
            pl.BlockSpec((tm, SSD_WIDTH), row),
            pl.BlockSpec((tm, SSD_WIDTH), row),
            pl.BlockSpec((tm, RET_V_WIDTH), row),
            pl.BlockSpec((tm, D_MODEL), row),
            vec(SSD_WIDTH), vec(SSD_WIDTH), vec(RET_V_WIDTH),
            _resident((EV_MIX, D_MODEL)),
            vec(D_MODEL), vec(D_MODEL),
            _resident((D_MODEL, FFN_HIDDEN)), _resident((FFN_HIDDEN, D_MODEL)),
            vec(D_MODEL),
        ],
        out_specs=pl.BlockSpec((tm, D_MODEL), row),
        out_shape=jax.ShapeDtypeStruct((t, D_MODEL), F32),
        compiler_params=_params("parallel"),
        name="ev_out_ffn",
    )(y, y, xbc_act2d, z, g, x2d, dskip, nssd, ngn, w_out, npost, nfpre, w1, w2, nfpost)


OD_Q0, OD_K0, OD_V0, OD_U0, OD_VG0, OD_END = 0, 1024, 2048, 3072, 3584, 4096


QKV_SLABS = 3 * ATT_WIDTH // LANES
DILATIONS = tuple(d for _, d in DILATED_PATTERNS)


def _od_inproj_kernel(x_ref, nw_ref, w_ref, gnw_ref, cos_ref, sin_ref, qkv_ref, qkv4_ref, qkv16_ref, u_ref, vn_ref,
                      slab_ref, slab4_ref):
    tm = x_ref.shape[0]
    hn = _rms(x_ref[...], nw_ref[...]).astype(BF16)

    def mm(c0, c1):
        return jnp.dot(hn, w_ref[:, c0:c1], preferred_element_type=F32)

    cos, sin = cos_ref[...], sin_ref[...]
    q_cols = [t * (ATT_HEAD_DIM ** -0.5 * LOG2E) for t in _rope_cols(mm(OD_Q0, OD_K0), cos, sin)]
    k_cols = _rope_cols(mm(OD_K0, OD_V0), cos, sin)
    v = mm(OD_V0, OD_U0)
    v_cols = [v[:, j * LANES:(j + 1) * LANES] for j in range(ATT_WIDTH // LANES)]
    for j, t in enumerate(q_cols + k_cols + v_cols):
        qkv_ref[:, j * LANES:(j + 1) * LANES] = t.astype(BF16)
        slab_ref[j] = t
    r4 = DILATIONS[1]
    n4 = tm // r4
    assert DILATIONS[2] == r4 * r4
    for j in range(QKV_SLABS):
        lanes = slice(j * LANES, (j + 1) * LANES)
        for c4 in range(r4):
            rows = slab_ref[j, pl.ds(c4, n4, stride=r4), :]
            qkv4_ref[c4, :, lanes] = rows.astype(BF16)
            slab4_ref[j, c4 * n4:(c4 + 1) * n4, :] = rows
        for c4 in range(r4):
            for c2 in range(r4):
                rows = slab4_ref[j, pl.ds(c4 * n4 + c2, n4 // r4, stride=r4), :]
                qkv16_ref[c4 + r4 * c2, :, lanes] = rows.astype(BF16)
    u_ref[...] = mm(OD_U0, OD_VG0).astype(BF16)
    vg = mm(OD_VG0, OD_END)
    mu = jnp.mean(vg, axis=-1, keepdims=True)
    cen = vg - mu
    var = jnp.mean(cen * cen, axis=-1, keepdims=True)
    vn_ref[...] = (cen * lax.rsqrt(var + NORM_EPS) * gnw_ref[...]).astype(BF16)


def _od_inproj(x3d, nw, w_in, gnw, cos, sin):
    b, s, _ = x3d.shape
    tm = ROW_TILE
    row = lambda bi, i: (bi, i, 0)
    const = lambda bi, i: (0, 0)
    r4, r16 = DILATIONS[1], DILATIONS[2]
    width = 3 * ATT_WIDTH
    return pl.pallas_call(
        _od_inproj_kernel,
        grid=(b, s // tm),
        in_specs=[
            pl.BlockSpec((None, tm, D_MODEL), row),
            pl.BlockSpec((1, D_MODEL), const),
            pl.BlockSpec(w_in.shape, const),
            pl.BlockSpec((1, GMLP_WIDTH), const),
            pl.BlockSpec((tm, LANES), lambda bi, i: (i, 0)),
            pl.BlockSpec((tm, LANES), lambda bi, i: (i, 0)),
        ],
        out_specs=[
            pl.BlockSpec((None, tm, width), row),
            pl.BlockSpec((None, r4, tm // r4, width), lambda bi, i: (bi, 0, i, 0)),
            pl.BlockSpec((None, r16, tm // r16, width), lambda bi, i: (bi, 0, i, 0)),
            pl.BlockSpec((None, tm, GMLP_WIDTH), row),
            pl.BlockSpec((None, tm, GMLP_WIDTH), row),
        ],
        out_shape=[jax.ShapeDtypeStruct((b, s, width), BF16),
                   jax.ShapeDtypeStruct((b, r4, s // r4, width), BF16),
                   jax.ShapeDtypeStruct((b, r16, s // r16, width), BF16),
                   jax.ShapeDtypeStruct((b, s, GMLP_WIDTH), BF16), jax.ShapeDtypeStruct((b, s, GMLP_WIDTH), BF16)],
        scratch_shapes=[pltpu.VMEM((QKV_SLABS, tm, LANES), F32), pltpu.VMEM((QKV_SLABS, tm, LANES), F32)],
        compiler_params=_params("parallel", "parallel"),
        name="od_inproj",
    )(x3d, nw, w_in, gnw, cos, sin)


ATT_Q_TILE = 128
ATT_GROUP = 2


def _attn_kernel(q_ref, kp_ref, kc_ref, kn_ref, vp_ref, vc_ref, vn_ref, o_ref, lse_ref, *, dilation):
    i = pl.program_id(1)
    last = pl.num_programs(1) - 1
    nk = ATT_Q_TILE + 2 * ATT_HALF
    qq = lax.broadcasted_iota(jnp.int32, (ATT_Q_TILE, nk), 0)
    kk = lax.broadcasted_iota(jnp.int32, (ATT_Q_TILE, nk), 1)
    k_lo = jnp.where(i > 0, 0, ATT_HALF)
    k_hi = jnp.where(i < last, nk - 1, ATT_HALF + ATT_Q_TILE - 1)
    valid = (kk >= jnp.maximum(qq, k_lo)) & (kk <= jnp.minimum(qq + 2 * ATT_HALF, k_hi))
    lane = lax.broadcasted_iota(jnp.int32, (ATT_Q_TILE, LANES), 1)
    first = lane < ATT_HEAD_DIM
    masks = (jnp.where(first[0:1, :], 1.0, 0.0).astype(BF16), jnp.where(first[0:1, :], 0.0, 1.0).astype(BF16))
    for j in range(ATT_GROUP):
        if dilation > 1:
            rows = pl.ds(pl.program_id(2) * ATT_GROUP + j, ATT_Q_TILE, stride=dilation)
        lse_all = jnp.zeros((ATT_Q_TILE, LANES), F32)
        for pr in range(ATT_HEADS // 2):
            cols = slice(pr * LANES, (pr + 1) * LANES)
            q2 = q_ref[j, :, cols]
            k2 = jnp.concatenate([kp_ref[j, :, cols], kc_ref[j, :, cols], kn_ref[j, :, cols]], axis=0)
            v2 = jnp.concatenate([vp_ref[j, :, cols], vc_ref[j, :, cols], vn_ref[j, :, cols]], axis=0)
            outs = []
            for half in range(2):
                s = lax.dot_general(q2 * masks[half], k2, (((1,), (1,)), ((), ())), preferred_element_type=F32)
                s = jnp.where(valid, s, NEG_BIG)
                m = jnp.max(s, axis=-1, keepdims=True)
                p = jnp.exp2(s - m)
                den = jnp.sum(p, axis=-1, keepdims=True)
                outs.append(jnp.dot(p.astype(BF16), v2, preferred_element_type=F32) * (1.0 / den))
                lse_all = jnp.where(lane == 2 * pr + half, m * LN2 + jnp.log(den), lse_all)
            o_pair = jnp.where(first, outs[0], outs[1])
            if dilation > 1:
                o_ref[pr, rows, :] = o_pair
            else:
                o_ref[j, pr, :, :] = o_pair
        if dilation > 1:
            lse_ref[rows, :] = lse_all
        else:
            lse_ref[j] = lse_all


def _attn(qkv, seq):
    b, r, l, _ = qkv.shape
    tq = ATT_Q_TILE
    per = tq // ATT_HALF
    nhalf = l // ATT_HALF
    pairs = ATT_HEADS // 2
    grp = ATT_GROUP
    if r > 1:
        grid = (b, l // tq, r // grp)
        blk = lambda rows: (None, grp, rows, ATT_WIDTH)
        at = lambda row_of: (lambda col: (lambda bi, i, c: (bi, c, row_of(i), col)))
        out_specs = [pl.BlockSpec((None, pairs, tq * r, LANES), lambda bi, i, c: (bi, 0, i, 0)),
                     pl.BlockSpec((None, tq * r, LANES), lambda bi, i, c: (bi, i, 0))]
    else:
        grid = (b // grp, l // tq, 1)
        blk = lambda rows: (grp, None, rows, ATT_WIDTH)
        at = lambda row_of: (lambda col: (lambda bi, i, c: (bi, 0, row_of(i), col)))
        out_specs = [pl.BlockSpec((grp, pairs, tq, LANES), lambda bi, i, c: (bi, 0, i, 0)),
                     pl.BlockSpec((grp, tq, LANES), lambda bi, i, c: (bi, i, 0))]
    main = lambda col: pl.BlockSpec(blk(tq), at(lambda i: i)(col))
    prev = lambda col: pl.BlockSpec(blk(ATT_HALF), at(lambda i: jnp.maximum(i * per - 1, 0))(col))
    nxt = lambda col: pl.BlockSpec(blk(ATT_HALF), at(lambda i: jnp.minimum((i + 1) * per, nhalf - 1))(col))
    return pl.pallas_call(
        functools.partial(_attn_kernel, dilation=r),
        grid=grid,
        in_specs=[main(0), prev(1), main(1), nxt(1), prev(2), main(2), nxt(2)],
        out_specs=out_specs,
        out_shape=[jax.ShapeDtypeStruct((b, pairs, seq, LANES), F32), jax.ShapeDtypeStruct((b, seq, LANES), F32)],
        compiler_params=_params("parallel", "parallel", "arbitrary"),
        name=f"od_attn_r{r}",
    )(qkv, qkv, qkv, qkv, qkv, qkv, qkv)


def _od_out_kernel(o1_ref, o2_ref, o3_ref, l1_ref, l2_ref, l3_ref, u_ref, vn_ref, ws_ref, bs_ref, x_ref, w_ref,
                   npost_ref, ex_ref, nfpre_ref, w1_ref, w2_ref, nfpost_ref, out_ref):
    first = lax.broadcasted_iota(jnp.int32, (CHUNK, LANES), 1) < GMLP_GROUP_DIM

    def per_head_lanes(w):
        hi = w.astype(BF16)
        lo = (w - hi.astype(F32)).astype(BF16)
        return (jnp.dot(hi, ex_ref[...], preferred_element_type=F32) + jnp.dot(lo, ex_ref[...], preferred_element_type=F32))

    def mix_rows(rows):
        def token_major(o_ref):
            return jnp.concatenate([o_ref[pr, rows, :] for pr in range(ATT_HEADS // 2)], axis=1)

        l1, l2, l3 = l1_ref[rows, :], l2_ref[rows, :], l3_ref[rows, :]
        m = jnp.maximum(jnp.maximum(l1, l2), l3)
        e1, e2, e3 = jnp.exp(l1 - m), jnp.exp(l2 - m), jnp.exp(l3 - m)
        inv = 1.0 / (e1 + e2 + e3)
        o3 = token_major(o3_ref)
        att = (o3 + per_head_lanes(e1 * inv) * (token_major(o1_ref) - o3)
               + per_head_lanes(e2 * inv) * (token_major(o2_ref) - o3))
        gate_rows = []
        for c0 in range(rows.start, rows.stop, CHUNK):
            crows = slice(c0, c0 + CHUNK)
            parts = []
            for pr in range(GMLP_GROUPS // 2):
                cols = slice(pr * LANES, (pr + 1) * LANES)
                vn2 = vn_ref[crows, cols]
                mixed = jnp.where(first, jnp.dot(ws_ref[2 * pr], vn2, preferred_element_type=F32),
                                  jnp.dot(ws_ref[2 * pr + 1], vn2, preferred_element_type=F32)) + bs_ref[:, cols]
                parts.append((u_ref[crows, cols].astype(F32) * mixed).astype(BF16))
            gate_rows.append(jnp.concatenate(parts, axis=1))
        cat = jnp.concatenate([att.astype(BF16), jnp.concatenate(gate_rows, axis=0)], axis=1)
        mix = jnp.dot(cat, w_ref[...], preferred_element_type=F32)
        return x_ref[rows, :] + _rms(mix, npost_ref[...])

    _mix_then_ffn(mix_rows, out_ref, nfpre_ref, w1_ref, w2_ref, nfpost_ref)


def _od_out(o1, o2, o3, l1, l2, l3, u, vn, ws, bs, x3d, w_out, npost, nfpre, w1, w2, nfpost):
    b, s, _ = x3d.shape
    tm = FFN_ROW_TILE
    row = lambda bi, i: (bi, i, 0)
    slabs = pl.BlockSpec((None, ATT_HEADS // 2, tm, LANES), lambda bi, i: (bi, 0, i, 0))
    head_of_lane = jnp.arange(ATT_WIDTH) // ATT_HEAD_DIM
    expand = (jnp.arange(LANES)[:, None] == head_of_lane[None, :]).astype(BF16)
    return pl.pallas_call(
        _od_out_kernel,
        grid=(b, s // tm),
        in_specs=[
            slabs, slabs, slabs,
            pl.BlockSpec((None, tm, LANES), row), pl.BlockSpec((None, tm, LANES), row), pl.BlockSpec((None, tm, LANES), row),
            pl.BlockSpec((None, tm, GMLP_WIDTH), row), pl.BlockSpec((None, tm, GMLP_WIDTH), row),
            _resident((GMLP_GROUPS, CHUNK, CHUNK)),
            _resident((CHUNK, GMLP_WIDTH)),
            pl.BlockSpec((None, tm, D_MODEL), row),
            _resident((OD_MIX, D_MODEL)),
            _resident((1, D_MODEL)),
            _resident((LANES, ATT_WIDTH)),
            _resident((1, D_MODEL)),
            _resident((D_MODEL, FFN_HIDDEN)), _resident((FFN_HIDDEN, D_MODEL)),
            _resident((1, D_MODEL)),
        ],
        out_specs=pl.BlockSpec((None, tm, D_MODEL), row),
        out_shape=jax.ShapeDtypeStruct((b, s, D_MODEL), F32),
        compiler_params=_params("parallel", "parallel"),
        name="od_out_ffn",
    )(o1, o2, o3, l1, l2, l3, u, vn, ws, bs, x3d, w_out, npost, expand, nfpre, w1, w2, nfpost)


def _rope_tables(seq):
    inv_freq = ROPE_THETA ** (-jnp.arange(0, ATT_HEAD_DIM, 2, dtype=F32) / ATT_HEAD_DIM)
    ang = jnp.arange(seq, dtype=F32)[:, None] * inv_freq[None, :]
    ang = jnp.concatenate([ang, ang, ang, ang], axis=-1)
    sign = jnp.where((jnp.arange(LANES) % 64) < 32, -1.0, 1.0).astype(F32)
    return jnp.cos(ang), jnp.sin(ang) * sign[None, :]


def _dt_lanes(per_dir):
    fwd = jnp.tile(per_dir[0], 3)
    bwd = jnp.tile(per_dir[1], 3)
    pad = jnp.zeros((64 - 3 * SSD_HEADS,), F32)
    return jnp.concatenate([fwd, pad, bwd, pad])[None, :]


def _prep_even(ev_in_proj, ev_conv_w, ev_conv_b, ssd_dt_bias, ssd_a_log, ssd_d, ssd_norm_w, ret_decay, ret_gn_w, ev_out_proj):
    wz, wxbc, wdt, wq, wk, wv, wg = jnp.split(ev_in_proj, [1024, 2560, 2592, 3104, 3616, 4640], axis=-1)
    w_main = jnp.concatenate([wz, wxbc, wq, wk, wv, wg], axis=-1).astype(BF16)
    wdt_f, wdt_b = wdt[:, :SSD_HEADS], wdt[:, SSD_HEADS:]
    zpad = jnp.zeros((D_MODEL, 64 - 3 * SSD_HEADS), F32)
    w_dt = jnp.concatenate([wdt_f, wdt_f, wdt_f, zpad, wdt_b, wdt_b, wdt_b, zpad], axis=-1).astype(BF16)
    conv_w = jnp.concatenate([ev_conv_w, jnp.zeros((8 - SSD_CONV, SSD_CONV_CH), F32)], axis=0)
    return dict(
        w_main=w_main, w_dt=w_dt, dt_bias=_dt_lanes(ssd_dt_bias), a_log=_dt_lanes(ssd_a_log),
        conv_w=conv_w, conv_b=ev_conv_b[None, :],
        dskip=jnp.repeat(ssd_d, SSD_HEAD_DIM)[None, :], nssd=ssd_norm_w[None, :], ngn=ret_gn_w[None, :],
        rd=jnp.broadcast_to(ret_decay[:, :, None], (2, RET_HEADS, LANES)),
        w_out=ev_out_proj.astype(BF16),
    )


def _even_layer(x2d, b, s, p, nmix_pre, nmix_post, ffn, cos, sin):
    z, xbc_act, q, k, v, g, dt, dtt = _ev_inproj(x2d.reshape(b, s, D_MODEL), nmix_pre, p["w_main"], p["w_dt"], p["dt_bias"],
                                                 p["a_log"], p["conv_w"], p["conv_b"], cos, sin)
    y = _ssd_ret(xbc_act, dt, dtt, q, k, v, p["rd"])
    t = b * s
    return _ev_out(y.reshape(2, t, EV_MIX), xbc_act.reshape(t, SSD_CONV_CH), z.reshape(t, SSD_WIDTH),
                   g.reshape(t, RET_V_WIDTH), x2d, p["dskip"], p["nssd"], p["ngn"], p["w_out"], nmix_post, *ffn)


def _odd_layer(x2d, b, s, p, nmix_pre, nmix_post, ffn, cos, sin):
    x3d = x2d.reshape(b, s, D_MODEL)
    qkv1, qkv4, qkv16, u, vn = _od_inproj(x3d, nmix_pre, p["w_in"], p["gnw"], cos, sin)
    (o1, l1), (o2, l2), (o3, l3) = [_attn(qkv, s) for qkv in (qkv1[:, None], qkv4, qkv16)]
    out = _od_out(o1, o2, o3, l1, l2, l3, u, vn, p["ws"], p["bs"], x3d, p["w_out"], nmix_post, *ffn)
    return out.reshape(b * s, D_MODEL)


def _trunk(x, even, odd, norm_mix_pre, norm_mix_post, norm_ffn_pre, norm_ffn_post, w1, w2):
    b, s, _ = x.shape
    cos, sin = _rope_tables(s)
    x2d = x.reshape(b * s, D_MODEL)
    depth = norm_mix_pre.shape[0]
    for i in range(depth):
        j = i // 2
        layer = _even_layer if i % 2 == 0 else _odd_layer
        params = even[j] if i % 2 == 0 else odd[j]
        ffn = (norm_ffn_pre[i][None, :], w1[i], w2[i], norm_ffn_post[i][None, :])
        x2d = layer(x2d, b, s, params, norm_mix_pre[i][None, :], norm_mix_post[i][None, :], ffn, cos, sin)
    return x2d.reshape(b, s, D_MODEL)


def kernel(x_prompt, x_sample, norm_mix_pre, norm_mix_post, norm_ffn_pre, norm_ffn_post, ffn_w1, ffn_w2, ev_in_proj, ev_conv_w, ev_conv_b, ssd_dt_bias, ssd_a_log, ssd_d, ssd_norm_w, ret_decay, ret_gn_w, ev_out_proj, od_in_proj, gmlp_norm_w, gmlp_ws, gmlp_bs, od_out_proj):
    even = [_prep_even(ev_in_proj[j], ev_conv_w[j], ev_conv_b[j], ssd_dt_bias[j], ssd_a_log[j], ssd_d[j], ssd_norm_w[j],
                       ret_decay[j], ret_gn_w[j], ev_out_proj[j]) for j in range(ev_in_proj.shape[0])]
    odd = [dict(w_in=od_in_proj[j].astype(BF16), gnw=gmlp_norm_w[j][None, :], ws=gmlp_ws[j].astype(BF16),
                bs=jnp.repeat(gmlp_bs[j].T, GMLP_GROUP_DIM, axis=1), w_out=od_out_proj[j].astype(BF16))
           for j in range(od_in_proj.shape[0])]
    w1 = ffn_w1.astype(BF16)
    w2 = ffn_w2.astype(BF16)
    run = functools.partial(_trunk, even=even, odd=odd, norm_mix_pre=norm_mix_pre, norm_mix_post=norm_mix_post,
                            norm_ffn_pre=norm_ffn_pre, norm_ffn_post=norm_ffn_post, w1=w1, w2=w2)
    return (run(x_prompt), run(x_sample))
```

```python
import functools
import math

import jax
import jax.numpy as jnp
from jax import lax
from jax.experimental import pallas as pl
from jax.experimental.pallas import tpu as pltpu

F32 = jnp.float32
BF16 = jnp.bfloat16

D_MODEL = 1024
NORM_EPS = 1e-6
ROPE_THETA = 10000.0
CHUNK = 128
LANES = 128

SSD_HEADS = 16
SSD_HEAD_DIM = 64
SSD_WIDTH = SSD_HEADS * SSD_HEAD_DIM
SSD_GROUPS = 2
SSD_HPG = SSD_HEADS // SSD_GROUPS
SSD_STATE = 128
SSD_CONV = 5
SSD_BC = 2 * SSD_GROUPS * SSD_STATE
SSD_CONV_CH = SSD_WIDTH + SSD_BC

RET_HEADS = 8
RET_QK_DIM = 64
RET_V_DIM = 128
RET_QK_WIDTH = RET_HEADS * RET_QK_DIM
RET_V_WIDTH = RET_HEADS * RET_V_DIM
EV_MIX = SSD_WIDTH + RET_V_WIDTH

ATT_HEADS = 16
ATT_HEAD_DIM = 64
ATT_WIDTH = ATT_HEADS * ATT_HEAD_DIM
DILATED_PATTERNS = ((128, 1), (512, 4), (2048, 16))
ATT_HALF = 64

GMLP_GROUPS = 8
GMLP_GROUP_DIM = 64
GMLP_WIDTH = GMLP_GROUPS * GMLP_GROUP_DIM
OD_MIX = ATT_WIDTH + GMLP_WIDTH
FFN_HIDDEN = 4 * D_MODEL

EV_Z0, EV_XBC0, EV_Q0, EV_K0, EV_V0, EV_G0, EV_END = 0, 1024, 2560, 3072, 3584, 4608, 5632
DT_LANE, CS_LANE, TOT_LANE = 0, 16, 32

ROW_TILE = 256
FFN_ROW_TILE = 512
SUB_ROWS = 256
CONV_HALO = 16
CONV_COLS = 256
VMEM_LIMIT = 56 * 1024 * 1024
NEG_BIG = -1e30
LOG2E = 1.4426950408889634
LN2 = 0.6931471805599453


def _params(*sem):
    return pltpu.CompilerParams(dimension_semantics=sem, vmem_limit_bytes=VMEM_LIMIT)


def _rms(x, w):
    return x * lax.rsqrt(jnp.mean(x * x, axis=-1, keepdims=True) + NORM_EPS) * w


def _silu(x):
    return x / (1.0 + jnp.exp(-x))


def _softplus(x):
    return jnp.maximum(x, 0.0) + jnp.log1p(jnp.exp(-jnp.abs(x)))


def _rope(t, cos, sin_signed, first_half):
    rot = jnp.where(first_half, pltpu.roll(t, 96, 1), pltpu.roll(t, 32, 1))
    return t * cos + rot * sin_signed


def _rope_cols(a, cos, sin_signed):
    lane = lax.broadcasted_iota(jnp.int32, (a.shape[0], LANES), 1)
    first_half = (lane % 64) < 32
    return [_rope(a[:, j * LANES:(j + 1) * LANES], cos, sin_signed, first_half) for j in range(a.shape[1] // LANES)]


def _split3(x):
    hi = x.astype(BF16)
    r1 = x - hi.astype(F32)
    mid = r1.astype(BF16)
    lo = (r1 - mid.astype(F32)).astype(BF16)
    return hi, mid, lo


def _ev_inproj_kernel(x_ref, xp_ref, xn_ref, nw_ref, w_ref, wdt_ref, dtb_ref, alog_ref, cw_ref, cb_ref, cos_ref, sin_ref,
                      z_ref, xbc_ref, q_ref, k_ref, v_ref, g_ref, dt_ref, dtt_ref, ext_ref):
    i = pl.program_id(1)
    last = pl.num_programs(1) - 1
    tm = x_ref.shape[0]
    nw = nw_ref[...]
    hn = _rms(x_ref[...], nw).astype(BF16)

    def mm(c0, c1):
        return jnp.dot(hn, w_ref[:, c0:c1], preferred_element_type=F32)

    cos, sin = cos_ref[...], sin_ref[...]

    def seg_z():
        z_ref[...] = mm(EV_Z0, EV_XBC0).astype(BF16)

    def seg_q():
        for j, t in enumerate(_rope_cols(mm(EV_Q0, EV_K0), cos, sin)):
            q_ref[:, j * LANES:(j + 1) * LANES] = t.astype(BF16)

    def seg_k():
        for j, t in enumerate(_rope_cols(mm(EV_K0, EV_V0), cos, sin)):
            k_ref[:, j * LANES:(j + 1) * LANES] = (t * (RET_QK_DIM ** -0.5)).astype(BF16)

    def seg_v():
        v_ref[...] = mm(EV_V0, EV_G0).astype(BF16)

    def seg_g():
        g_ref[...] = mm(EV_G0, EV_END).astype(BF16)

    hn_ext = jnp.concatenate([_rms(xp_ref[...], nw).astype(BF16), hn, _rms(xn_ref[...], nw).astype(BF16)], axis=0)
    erow = lax.broadcasted_iota(jnp.int32, (tm + 2 * CONV_HALO, 1), 0)
    lo = jnp.where(i > 0, 0, CONV_HALO)
    hi = jnp.where(i < last, tm + 2 * CONV_HALO, tm + CONV_HALO)
    inside = (erow >= lo) & (erow < hi)
    others = [seg_z, seg_q, seg_k, seg_v, seg_g]
    for blk, c0 in enumerate(range(0, SSD_CONV_CH, CONV_COLS)):
        cols = slice(c0, c0 + CONV_COLS)
        ext = jnp.dot(hn_ext, w_ref[:, EV_XBC0 + c0:EV_XBC0 + c0 + CONV_COLS], preferred_element_type=F32)
        ext_ref[blk % 2] = jnp.where(inside, ext, 0.0)
        acc = jnp.broadcast_to(cb_ref[:, cols], (tm, CONV_COLS))
        for j in range(SSD_CONV):
            off = CONV_HALO - SSD_CONV // 2 + j
            acc = acc + cw_ref[j:j + 1, cols] * ext_ref[blk % 2, off:off + tm, :]
        xbc_ref[:, cols] = _silu(acc).astype(BF16)
        if blk < len(others):
            others[blk]()

    raw = jnp.dot(hn, wdt_ref[...], preferred_element_type=F32)
    dt = _softplus(raw + dtb_ref[...])
    la = dt * (-jnp.exp(alog_ref[...]))
    lane = lax.broadcasted_iota(jnp.int32, (CHUNK, LANES), 1)
    sub = lane % 64
    is_dt = sub < CS_LANE
    is_cs = (sub >= CS_LANE) & (sub < TOT_LANE)
    is_bwd = lane >= 64
    row = lax.broadcasted_iota(jnp.int32, (CHUNK, CHUNK), 0)
    col = lax.broadcasted_iota(jnp.int32, (CHUNK, CHUNK), 1)
    tri = jnp.where(row >= col, 1.0, 0.0).astype(BF16)
    for c in range(tm // CHUNK):
        rows = slice(c * CHUNK, (c + 1) * CHUNK)
        la_c = la[rows]
        cs = sum(jnp.dot(tri, part, preferred_element_type=F32) for part in _split3(la_c))
        tot = jnp.broadcast_to(cs[CHUNK - 1:CHUNK, :], cs.shape)
        rcs = tot - cs + la_c
        rec = jnp.where(is_dt, dt[rows], jnp.where(is_cs, jnp.where(is_bwd, rcs, cs), tot))
        rec_b = pltpu.roll(rec, 64, 1)
        dt_ref[0, rows, :] = rec
        dt_ref[1, rows, :] = rec_b
        dtt_ref[0, rows, :] = rec.T
        dtt_ref[1, rows, :] = rec_b.T


def _ev_inproj(x3d, nw, w_main, w_dt, dt_bias, a_log, conv_w, conv_b, cos, sin):
    b, s, _ = x3d.shape
    tm = ROW_TILE
    per = tm // CONV_HALO
    nh = s // CONV_HALO
    row = lambda bi, i: (bi, i, 0)
    const = lambda bi, i: (0, 0)
    bf = lambda n: jax.ShapeDtypeStruct((b, s, n), BF16)
    rec = pl.BlockSpec((2, None, tm, LANES), lambda bi, i: (0, bi, i, 0))
    return pl.pallas_call(
        _ev_inproj_kernel,
        grid=(b, s // tm),
        in_specs=[
            pl.BlockSpec((None, tm, D_MODEL), row),
            pl.BlockSpec((None, CONV_HALO, D_MODEL), lambda bi, i: (bi, jnp.maximum(i * per - 1, 0), 0)),
            pl.BlockSpec((None, CONV_HALO, D_MODEL), lambda bi, i: (bi, jnp.minimum((i + 1) * per, nh - 1), 0)),
            pl.BlockSpec((1, D_MODEL), const),
            pl.BlockSpec(w_main.shape, const),
            pl.BlockSpec(w_dt.shape, const),
            pl.BlockSpec((1, LANES), const),
            pl.BlockSpec((1, LANES), const),
            pl.BlockSpec(conv_w.shape, const),
            pl.BlockSpec((1, SSD_CONV_CH), const),
            pl.BlockSpec((tm, LANES), lambda bi, i: (i, 0)),
            pl.BlockSpec((tm, LANES), lambda bi, i: (i, 0)),
        ],
        out_specs=[
            pl.BlockSpec((None, tm, SSD_WIDTH), row),
            pl.BlockSpec((None, tm, SSD_CONV_CH), row),
            pl.BlockSpec((None, tm, RET_QK_WIDTH), row),
            pl.BlockSpec((None, tm, RET_QK_WIDTH), row),
            pl.BlockSpec((None, tm, RET_V_WIDTH), row),
            pl.BlockSpec((None, tm, RET_V_WIDTH), row),
            rec, rec,
        ],
        out_shape=[bf(SSD_WIDTH), bf(SSD_CONV_CH), bf(RET_QK_WIDTH), bf(RET_QK_WIDTH), bf(RET_V_WIDTH), bf(RET_V_WIDTH),
                   jax.ShapeDtypeStruct((2, b, s, LANES), F32), jax.ShapeDtypeStruct((2, b, s, LANES), F32)],
        scratch_shapes=[pltpu.VMEM((2, tm + 2 * CONV_HALO, CONV_COLS), F32)],
        compiler_params=_params("parallel", "parallel"),
        name="ev_inproj",
    )(x3d, x3d, x3d, nw, w_main, w_dt, dt_bias, a_log, conv_w, conv_b, cos, sin)


SSD_STEP_CHUNKS = 2


def _ssd_ret_kernel(xs_ref, bc_ref, dt_ref, dtt_ref, q_ref, k_ref, v_ref, rd_ref, y_out_ref, h_ref, r_ref, dm_ref, dec_ref):
    d = pl.program_id(1)
    step = pl.program_id(2)
    fwd = d == 0
    row = lax.broadcasted_iota(jnp.int32, (CHUNK, CHUNK), 0)
    col = lax.broadcasted_iota(jnp.int32, (CHUNK, CHUNK), 1)
    ahead = jnp.where(fwd, row - col, col - row)
    causal = ahead >= 0
    first = col < SSD_HEAD_DIM
    first_row = first[0:1, :]
    m_lo = jnp.where(first_row, 1.0, 0.0).astype(BF16)
    m_hi = jnp.where(first_row, 0.0, 1.0).astype(BF16)

    @pl.when(step == 0)
    def _():
        h_ref[...] = jnp.zeros_like(h_ref)
        r_ref[...] = jnp.zeros_like(r_ref)
        rd = rd_ref[...]
        lg = jnp.minimum(rd, 0.0) - jnp.log1p(jnp.exp(-jnp.abs(rd)))
        rowf = row.astype(F32)
        dist = ahead.astype(F32)
        posq = jnp.where(fwd, rowf + 1.0, CHUNK - rowf)
        posk = jnp.where(fwd, CHUNK - 1.0 - rowf, rowf)
        for h in range(RET_HEADS):
            dm_ref[h] = jnp.where(causal, jnp.exp(lg[h:h + 1, :] * dist), 0.0)
        for pr in range(RET_HEADS // 2):
            lg_a, lg_b = lg[2 * pr:2 * pr + 1, :], lg[2 * pr + 1:2 * pr + 2, :]
            lg2 = jnp.where(first_row, lg_a, lg_b)
            dec_ref[0, pr] = jnp.exp(lg2 * posq)
            dec_ref[1, pr] = jnp.exp(lg2 * posk)
            dec_ref[2, pr] = jnp.exp(jnp.where(row < RET_QK_DIM, lg_a, lg_b) * float(CHUNK))

    nt = (((1,), (1,)), ((), ()))

    def ret_pair(pr, rows, y_ref):
        cols = slice(pr * LANES, (pr + 1) * LANES)
        q2 = q_ref[rows, cols]
        k2 = k_ref[rows, cols]
        rp = r_ref[pr]
        rpb = rp.astype(BF16)
        qd2 = (q2.astype(F32) * dec_ref[0, pr]).astype(BF16)
        kdt = (k2.astype(F32) * dec_ref[1, pr]).T.astype(BF16)
        st = []
        for half, msk in enumerate((m_lo, m_hi)):
            h = 2 * pr + half
            vh = v_ref[rows, h * RET_V_DIM:(h + 1) * RET_V_DIM]
            sc = lax.dot_general(q2 * msk, k2, nt, preferred_element_type=F32) * dm_ref[h]
            y = (jnp.dot(sc.astype(BF16), vh, preferred_element_type=F32)
                 + jnp.dot(qd2 * msk, rpb, preferred_element_type=F32))
            y_ref[rows, SSD_WIDTH + h * RET_V_DIM:SSD_WIDTH + (h + 1) * RET_V_DIM] = y.astype(BF16)
            st.append(jnp.dot(kdt[half * RET_QK_DIM:(half + 1) * RET_QK_DIM, :], vh, preferred_element_type=F32))
        r_ref[pr] = rp * dec_ref[2, pr] + jnp.concatenate(st, axis=0)

    def ssd_pair(g, pr, cb, hg, yoff, bgt, rows, rec2, rect, rect2, y_ref):
        cols = slice((g * SSD_HPG // 2 + pr) * LANES, (g * SSD_HPG // 2 + pr + 1) * LANES)
        lanes = slice(pr * LANES, (pr + 1) * LANES)
        x2 = xs_ref[rows, cols]
        yd, csb, st, dec = [], [], [], []
        for half in range(2):
            e = g * SSD_HPG + 2 * pr + half
            cs_row = rect2[CS_LANE + e:CS_LANE + e + 1, :]
            dt_row = rect[DT_LANE + e:DT_LANE + e + 1, :]
            tot_row = rect2[TOT_LANE + e:TOT_LANE + e + 1, :]
            csb.append(jnp.broadcast_to(rec2[:, CS_LANE + e:CS_LANE + e + 1], (CHUNK, CHUNK)))
            decay = jnp.where(causal, jnp.exp2(csb[half] - cs_row), 0.0)
            me = (cb * decay * dt_row).astype(BF16)
            yd.append(jnp.dot(me, x2, preferred_element_type=F32))
            bw = (bgt * (dt_row * jnp.exp2(tot_row - cs_row))).astype(BF16)
            st.append(jnp.dot(bw, x2, preferred_element_type=F32))
            dec.append(jnp.exp2(tot_row))
        y = jnp.where(first, yd[0], yd[1]) + yoff[:, lanes] * jnp.exp2(jnp.where(first, csb[0], csb[1]))
        y_ref[rows, cols] = y.astype(BF16)
        h_ref[g, :, lanes] = hg[:, lanes] * jnp.where(first_row, dec[0], dec[1]) + jnp.where(first, st[0], st[1])

    def one_chunk(rows):
        rec2 = dt_ref[rows, :] * LOG2E
        rect = dtt_ref[rows, :]
        rect2 = rect * LOG2E
        for g in range(SSD_GROUPS):
            bg = bc_ref[rows, g * SSD_STATE:(g + 1) * SSD_STATE]
            cg = bc_ref[rows, (SSD_GROUPS + g) * SSD_STATE:(SSD_GROUPS + g + 1) * SSD_STATE]
            cb = lax.dot_general(cg, bg, nt, preferred_element_type=F32)
            hg = h_ref[g]
            yoff = jnp.dot(cg, hg.astype(BF16), preferred_element_type=F32)
            bgt = bg.astype(F32).T
            for pr in range(SSD_HPG // 2):
                if pr % 2 == 0:
                    ret_pair(g * 2 + pr // 2, rows, y_out_ref)
                ssd_pair(g, pr, cb, hg, yoff, bgt, rows, rec2, rect, rect2, y_out_ref)

    for k in range(SSD_STEP_CHUNKS):
        sub = jnp.where(fwd, k, SSD_STEP_CHUNKS - 1 - k)
        one_chunk(pl.ds(pl.multiple_of(sub * CHUNK, CHUNK), CHUNK))


def _ssd_ret(xbc_act, dt, dtt, q, k, v, rd):
    b, s, _ = xbc_act.shape
    rows = SSD_STEP_CHUNKS * CHUNK
    nc = s // rows

    def chunk(d, i):
        return i + d * (nc - 1 - 2 * i)

    seq = lambda bi, d, i: (bi, chunk(d, i), 0)
    rec = lambda bi, d, i: (d, bi, chunk(d, i), 0)
    return pl.pallas_call(
        _ssd_ret_kernel,
        grid=(b, 2, nc),
        in_specs=[
            pl.BlockSpec((None, rows, SSD_WIDTH), seq),
            pl.BlockSpec((None, rows, SSD_BC), lambda bi, d, i: (bi, chunk(d, i), SSD_WIDTH // SSD_BC)),
            pl.BlockSpec((None, None, rows, LANES), rec),
            pl.BlockSpec((None, None, rows, LANES), rec),
            pl.BlockSpec((None, rows, RET_QK_WIDTH), seq),
            pl.BlockSpec((None, rows, RET_QK_WIDTH), seq),
            pl.BlockSpec((None, rows, RET_V_WIDTH), seq),
            pl.BlockSpec((None, RET_HEADS, LANES), lambda bi, d, i: (d, 0, 0)),
        ],
        out_specs=pl.BlockSpec((None, None, rows, EV_MIX), rec),
        out_shape=jax.ShapeDtypeStruct((2, b, s, EV_MIX), BF16),
        scratch_shapes=[pltpu.VMEM((SSD_GROUPS, SSD_STATE, SSD_HPG * SSD_HEAD_DIM), F32),
                        pltpu.VMEM((RET_HEADS // 2, 2 * RET_QK_DIM, RET_V_DIM), F32),
                        pltpu.VMEM((RET_HEADS, CHUNK, CHUNK), F32),
                        pltpu.VMEM((3, RET_HEADS // 2, CHUNK, CHUNK), F32)],
        compiler_params=_params("parallel", "arbitrary", "arbitrary"),
        name="ev_ssd_ret",
    )(xbc_act, xbc_act, dt, dtt, q, k, v, rd)


FFN_HIDDEN_TILE = 1024


def _ffn_chunk(hn, w1_ref, w2_ref, c):
    cols = slice(c * FFN_HIDDEN_TILE, (c + 1) * FFN_HIDDEN_TILE)
    a = jnp.maximum(jnp.dot(hn, w1_ref[:, cols], preferred_element_type=F32), 0.0)
    return jnp.dot((a * a).astype(BF16), w2_ref[cols, :], preferred_element_type=F32)


def _mix_then_ffn(mix_rows, o_ref, nfpre_ref, w1_ref, w2_ref, nfpost_ref):
    n = o_ref.shape[0]
    half = n // 2
    chunks = range(FFN_HIDDEN // FFN_HIDDEN_TILE)
    xa = mix_rows(0, half)
    hna = _rms(xa, nfpre_ref[...]).astype(BF16)
    acc = _ffn_chunk(hna, w1_ref, w2_ref, 0)
    xb = mix_rows(half, half)
    for c in chunks[1:]:
        acc = acc + _ffn_chunk(hna, w1_ref, w2_ref, c)
    o_ref[0:half, :] = xa + _rms(acc, nfpost_ref[...])
    hnb = _rms(xb, nfpre_ref[...]).astype(BF16)
    acc = _ffn_chunk(hnb, w1_ref, w2_ref, 0)
    for c in chunks[1:]:
        acc = acc + _ffn_chunk(hnb, w1_ref, w2_ref, c)
    o_ref[half:n, :] = xb + _rms(acc, nfpost_ref[...])


def _resident(shape):
    return pl.BlockSpec(shape, lambda *_: (0,) * len(shape), pipeline_mode=pl.Buffered(1))


def _ev_out_kernel(yf_ref, yb_ref, xs_ref, z_ref, g_ref, x_ref, dskip_ref, nssd_ref, ngn_ref, w_ref, npost_ref,
                   nfpre_ref, w1_ref, w2_ref, nfpost_ref, o_ref):
    gw = SSD_WIDTH // SSD_GROUPS

    def mix_rows(start, size):
        rows = pl.ds(start, size)
        yf = yf_ref[rows, :].astype(F32)
        yb = yb_ref[rows, :].astype(F32)
        y = yf[:, :SSD_WIDTH] + yb[:, :SSD_WIDTH] + xs_ref[rows, :].astype(F32) * dskip_ref[...]
        gg = y * _silu(z_ref[rows, :].astype(F32))
        parts = []
        for g in range(SSD_GROUPS):
            part = gg[:, g * gw:(g + 1) * gw]
            part = (part * lax.rsqrt(jnp.mean(part * part, axis=-1, keepdims=True) + NORM_EPS)
                    * nssd_ref[:, g * gw:(g + 1) * gw])
            parts.append(part.astype(BF16))
        r = yf[:, SSD_WIDTH:] + yb[:, SSD_WIDTH:]
        gate = _silu(g_ref[rows, :].astype(F32))
        for h in range(RET_HEADS):
            lanes = slice(h * RET_V_DIM, (h + 1) * RET_V_DIM)
            part = r[:, lanes]
            mu = jnp.mean(part, axis=-1, keepdims=True)
            cen = part - mu
            var = jnp.mean(cen * cen, axis=-1, keepdims=True)
            parts.append((cen * lax.rsqrt(var + NORM_EPS) * ngn_ref[:, lanes] * gate[:, lanes]).astype(BF16))
        mix = jnp.dot(jnp.concatenate(parts, axis=1), w_ref[...], preferred_element_type=F32)
        return x_ref[rows, :] + _rms(mix, npost_ref[...])

    _mix_then_ffn(mix_rows, o_ref, nfpre_ref, w1_ref, w2_ref, nfpost_ref)


def _ev_out(y, xbc_act2d, z, g, x2d, dskip, nssd, ngn, w_out, npost, nfpre, w1, w2, nfpost):
    t = x2d.shape[0]
    tm = FFN_ROW_TILE
    row = lambda i: (i, 0)
    vec = lambda w: _resident((1, w))
    return pl.pallas_call(
        _ev_out_kernel,
        grid=(t // tm,),
        in_specs=[
            pl.BlockSpec((None, tm, EV_MIX), lambda i: (0, i, 0)),
            pl.BlockSpec((None, tm, EV_MIX), lambda i: (1, i, 0)),
            pl.BlockSpec((tm, SSD_WIDTH), row),
            pl.BlockSpec((tm, SSD_WIDTH), row),
            pl.BlockSpec((tm, RET_V_WIDTH), row),
            pl.BlockSpec((tm, D_MODEL), row),
            vec(SSD_WIDTH), vec(SSD_WIDTH), vec(RET_V_WIDTH),
            _resident((EV_MIX, D_MODEL)),
            vec(D_MODEL), vec(D_MODEL),
            _resident((D_MODEL, FFN_HIDDEN)), _resident((FFN_HIDDEN, D_MODEL)),
            vec(D_MODEL),
        ],
        out_specs=pl.BlockSpec((tm, D_MODEL), row),
        out_shape=jax.ShapeDtypeStruct((t, D_MODEL), F32),
        compiler_params=_params("parallel"),
        name="ev_out_ffn",
    )(y, y, xbc_act2d, z, g, x2d, dskip, nssd, ngn, w_out, npost, nfpre, w1, w2, nfpost)


OD_Q0, OD_K0, OD_V0, OD_U0, OD_VG0, OD_END = 0, 1024, 2048, 3072, 3584, 4096


QKV_SLABS = 3 * ATT_WIDTH // LANES
DILATIONS = tuple(d for _, d in DILATED_PATTERNS)
assert DILATIONS[0] == 1 and DILATIONS[2] == DILATIONS[1] ** 2


def _od_inproj_kernel(x_ref, nw_ref, w_ref, gnw_ref, cos_ref, sin_ref, qkv_ref, qkv4_ref, qkv16_ref, u_ref, vn_ref,
                      slab_ref, slab4_ref):
    tm = x_ref.shape[0]
    hn = _rms(x_ref[...], nw_ref[...]).astype(BF16)

    def mm(c0, c1):
        return jnp.dot(hn, w_ref[:, c0:c1], preferred_element_type=F32)

    cos, sin = cos_ref[...], sin_ref[...]
    r4 = DILATIONS[1]
    n4 = tm // r4

    def emit(j0, cols):
        for j, t in enumerate(cols, start=j0):
            lanes = slice(j * LANES, (j + 1) * LANES)
            qkv_ref[:, lanes] = t.astype(BF16)
            slab_ref[j] = t
            for c4 in range(r4):
                rows = slab_ref[j, pl.ds(c4, n4, stride=r4), :]
                qkv4_ref[c4, :, lanes] = rows.astype(BF16)
                slab4_ref[j, c4 * n4:(c4 + 1) * n4, :] = rows
            for c4 in range(r4):
                for c2 in range(r4):
                    rows = slab4_ref[j, pl.ds(c4 * n4 + c2, n4 // r4, stride=r4), :]
                    qkv16_ref[c4 + r4 * c2, :, lanes] = rows.astype(BF16)

    per = ATT_WIDTH // LANES
    emit(0, [t * (ATT_HEAD_DIM ** -0.5 * LOG2E) for t in _rope_cols(mm(OD_Q0, OD_K0), cos, sin)])
    emit(per, _rope_cols(mm(OD_K0, OD_V0), cos, sin))
    v = mm(OD_V0, OD_U0)
    emit(2 * per, [v[:, j * LANES:(j + 1) * LANES] for j in range(per)])
    u_ref[...] = mm(OD_U0, OD_VG0).astype(BF16)
    vg = mm(OD_VG0, OD_END)
    mu = jnp.mean(vg, axis=-1, keepdims=True)
    cen = vg - mu
    var = jnp.mean(cen * cen, axis=-1, keepdims=True)
    vn_ref[...] = (cen * lax.rsqrt(var + NORM_EPS) * gnw_ref[...]).astype(BF16)


def _od_inproj(x3d, nw, w_in, gnw, cos, sin):
    b, s, _ = x3d.shape
    tm = ROW_TILE
    row = lambda bi, i: (bi, i, 0)
    const = lambda bi, i: (0, 0)
    r4, r16 = DILATIONS[1], DILATIONS[2]
    width = 3 * ATT_WIDTH
    return pl.pallas_call(
        _od_inproj_kernel,
        grid=(b, s // tm),
        in_specs=[
            pl.BlockSpec((None, tm, D_MODEL), row),
            pl.BlockSpec((1, D_MODEL), const),
            pl.BlockSpec(w_in.shape, const),
            pl.BlockSpec((1, GMLP_WIDTH), const),
            pl.BlockSpec((tm, LANES), lambda bi, i: (i, 0)),
            pl.BlockSpec((tm, LANES), lambda bi, i: (i, 0)),
        ],
        out_specs=[
            pl.BlockSpec((None, tm, width), row),
            pl.BlockSpec((None, r4, tm // r4, width), lambda bi, i: (bi, 0, i, 0)),
            pl.BlockSpec((None, r16, tm // r16, width), lambda bi, i: (bi, 0, i, 0)),
            pl.BlockSpec((None, tm, GMLP_WIDTH), row),
            pl.BlockSpec((None, tm, GMLP_WIDTH), row),
        ],
        out_shape=[jax.ShapeDtypeStruct((b, s, width), BF16),
                   jax.ShapeDtypeStruct((b, r4, s // r4, width), BF16),
                   jax.ShapeDtypeStruct((b, r16, s // r16, width), BF16),
                   jax.ShapeDtypeStruct((b, s, GMLP_WIDTH), BF16), jax.ShapeDtypeStruct((b, s, GMLP_WIDTH), BF16)],
        scratch_shapes=[pltpu.VMEM((QKV_SLABS, tm, LANES), F32), pltpu.VMEM((QKV_SLABS, tm, LANES), F32)],
        compiler_params=_params("parallel", "parallel"),
        name="od_inproj",
    )(x3d, nw, w_in, gnw, cos, sin)


ATT_Q_TILE = 128
ATT_GROUP = 2


def _attn_kernel(q_ref, kp_ref, kc_ref, kn_ref, vp_ref, vc_ref, vn_ref, o_ref, lse_ref, *, dilation):
    i = pl.program_id(1)
    last = pl.num_programs(1) - 1
    nk = ATT_Q_TILE + 2 * ATT_HALF
    qq = lax.broadcasted_iota(jnp.int32, (ATT_Q_TILE, nk), 0)
    kk = lax.broadcasted_iota(jnp.int32, (ATT_Q_TILE, nk), 1)
    k_lo = jnp.where(i > 0, 0, ATT_HALF)
    k_hi = jnp.where(i < last, nk - 1, ATT_HALF + ATT_Q_TILE - 1)
    valid = (kk >= jnp.maximum(qq, k_lo)) & (kk <= jnp.minimum(qq + 2 * ATT_HALF, k_hi))
    lane = lax.broadcasted_iota(jnp.int32, (ATT_Q_TILE, LANES), 1)
    first = lane < ATT_HEAD_DIM
    masks = (jnp.where(first[0:1, :], 1.0, 0.0).astype(BF16), jnp.where(first[0:1, :], 0.0, 1.0).astype(BF16))
    for j in range(ATT_GROUP):
        if dilation > 1:
            rows = pl.ds(pl.program_id(2) * ATT_GROUP + j, ATT_Q_TILE, stride=dilation)
        lse_all = jnp.zeros((ATT_Q_TILE, LANES), F32)
        for pr in range(ATT_HEADS // 2):
            cols = slice(pr * LANES, (pr + 1) * LANES)
            q2 = q_ref[j, :, cols]
            k2 = jnp.concatenate([kp_ref[j, :, cols], kc_ref[j, :, cols], kn_ref[j, :, cols]], axis=0)
            v2 = jnp.concatenate([vp_ref[j, :, cols], vc_ref[j, :, cols], vn_ref[j, :, cols]], axis=0)
            outs = []
            for half in range(2):
                s = lax.dot_general(q2 * masks[half], k2, (((1,), (1,)), ((), ())), preferred_element_type=F32)
                s = jnp.where(valid, s, NEG_BIG)
                m = jnp.max(s, axis=-1, keepdims=True)
                p = jnp.exp2(s - m)
                den = jnp.sum(p, axis=-1, keepdims=True)
                outs.append(jnp.dot(p.astype(BF16), v2, preferred_element_type=F32) * (1.0 / den))
                lse_all = jnp.where(lane == 2 * pr + half, m * LN2 + jnp.log(den), lse_all)
            o_pair = jnp.where(first, outs[0], outs[1])
            if dilation > 1:
                o_ref[pr, rows, :] = o_pair
            else:
                o_ref[j, pr, :, :] = o_pair
        if dilation > 1:
            lse_ref[rows, :] = lse_all
        else:
            lse_ref[j] = lse_all


def _attn(qkv, seq):
    b, r, l, _ = qkv.shape
    tq = ATT_Q_TILE
    per = tq // ATT_HALF
    nhalf = l // ATT_HALF
    pairs = ATT_HEADS // 2
    grp = ATT_GROUP
    if r > 1:
        grid = (b, l // tq, r // grp)
        blk = lambda rows: (None, grp, rows, ATT_WIDTH)
        at = lambda row_of: (lambda col: (lambda bi, i, c: (bi, c, row_of(i), col)))
        out_specs = [pl.BlockSpec((None, pairs, tq * r, LANES), lambda bi, i, c: (bi, 0, i, 0)),
                     pl.BlockSpec((None, tq * r, LANES), lambda bi, i, c: (bi, i, 0))]
    else:
        grid = (b // grp, l // tq, 1)
        blk = lambda rows: (grp, None, rows, ATT_WIDTH)
        at = lambda row_of: (lambda col: (lambda bi, i, c: (bi, 0, row_of(i), col)))
        out_specs = [pl.BlockSpec((grp, pairs, tq, LANES), lambda bi, i, c: (bi, 0, i, 0)),
                     pl.BlockSpec((grp, tq, LANES), lambda bi, i, c: (bi, i, 0))]
    main = lambda col: pl.BlockSpec(blk(tq), at(lambda i: i)(col))
    prev = lambda col: pl.BlockSpec(blk(ATT_HALF), at(lambda i: jnp.maximum(i * per - 1, 0))(col))
    nxt = lambda col: pl.BlockSpec(blk(ATT_HALF), at(lambda i: jnp.minimum((i + 1) * per, nhalf - 1))(col))
    return pl.pallas_call(
        functools.partial(_attn_kernel, dilation=r),
        grid=grid,
        in_specs=[main(0), prev(1), main(1), nxt(1), prev(2), main(2), nxt(2)],
        out_specs=out_specs,
        out_shape=[jax.ShapeDtypeStruct((b, pairs, seq, LANES), F32), jax.ShapeDtypeStruct((b, seq, LANES), F32)],
        compiler_params=_params("parallel", "parallel", "arbitrary"),
        name=f"od_attn_r{r}",
    )(qkv, qkv, qkv, qkv, qkv, qkv, qkv)


def _od_out_kernel(o1_ref, o2_ref, o3_ref, l1_ref, l2_ref, l3_ref, u_ref, vn_ref, ws_ref, bs_ref, x_ref, w_ref,
                   npost_ref, ex_ref, nfpre_ref, w1_ref, w2_ref, nfpost_ref, out_ref):
    first = lax.broadcasted_iota(jnp.int32, (CHUNK, LANES), 1) < GMLP_GROUP_DIM

    def per_head_lanes(w):
        hi = w.astype(BF16)
        lo = (w - hi.astype(F32)).astype(BF16)
        return (jnp.dot(hi, ex_ref[...], preferred_element_type=F32) + jnp.dot(lo, ex_ref[...], preferred_element_type=F32))

    def mix_rows(start, size):
        rows = pl.ds(start, size)

        def token_major(o_ref):
            return jnp.concatenate([o_ref[pr, rows, :] for pr in range(ATT_HEADS // 2)], axis=1)

        l1, l2, l3 = l1_ref[rows, :], l2_ref[rows, :], l3_ref[rows, :]
        m = jnp.maximum(jnp.maximum(l1, l2), l3)
        e1, e2, e3 = jnp.exp(l1 - m), jnp.exp(l2 - m), jnp.exp(l3 - m)
        inv = 1.0 / (e1 + e2 + e3)
        o3 = token_major(o3_ref)
        att = (o3 + per_head_lanes(e1 * inv) * (token_major(o1_ref) - o3)
               + per_head_lanes(e2 * inv) * (token_major(o2_ref) - o3))
        gate_rows = []
        for c0 in range(0, size, CHUNK):
            crows = pl.ds(start + c0, CHUNK)
            parts = []
            for pr in range(GMLP_GROUPS // 2):
                cols = slice(pr * LANES, (pr + 1) * LANES)
                vn2 = vn_ref[crows, cols]
                mixed = jnp.where(first, jnp.dot(ws_ref[2 * pr], vn2, preferred_element_type=F32),
                                  jnp.dot(ws_ref[2 * pr + 1], vn2, preferred_element_type=F32)) + bs_ref[:, cols]
                parts.append((u_ref[crows, cols].astype(F32) * mixed).astype(BF16))
            gate_rows.append(jnp.concatenate(parts, axis=1))
        cat = jnp.concatenate([att.astype(BF16), jnp.concatenate(gate_rows, axis=0)], axis=1)
        mix = jnp.dot(cat, w_ref[...], preferred_element_type=F32)
        return x_ref[rows, :] + _rms(mix, npost_ref[...])

    _mix_then_ffn(mix_rows, out_ref, nfpre_ref, w1_ref, w2_ref, nfpost_ref)


def _od_out(o1, o2, o3, l1, l2, l3, u, vn, ws, bs, x3d, w_out, npost, nfpre, w1, w2, nfpost):
    b, s, _ = x3d.shape
    tm = FFN_ROW_TILE
    row = lambda bi, i: (bi, i, 0)
    slabs = pl.BlockSpec((None, ATT_HEADS // 2, tm, LANES), lambda bi, i: (bi, 0, i, 0))
    head_of_lane = jnp.arange(ATT_WIDTH) // ATT_HEAD_DIM
    expand = (jnp.arange(LANES)[:, None] == head_of_lane[None, :]).astype(BF16)
    return pl.pallas_call(
        _od_out_kernel,
        grid=(b, s // tm),
        in_specs=[
            slabs, slabs, slabs,
            pl.BlockSpec((None, tm, LANES), row), pl.BlockSpec((None, tm, LANES), row), pl.BlockSpec((None, tm, LANES), row),
            pl.BlockSpec((None, tm, GMLP_WIDTH), row), pl.BlockSpec((None, tm, GMLP_WIDTH), row),
            _resident((GMLP_GROUPS, CHUNK, CHUNK)),
            _resident((CHUNK, GMLP_WIDTH)),
            pl.BlockSpec((None, tm, D_MODEL), row),
            _resident((OD_MIX, D_MODEL)),
            _resident((1, D_MODEL)),
            _resident((LANES, ATT_WIDTH)),
            _resident((1, D_MODEL)),
            _resident((D_MODEL, FFN_HIDDEN)), _resident((FFN_HIDDEN, D_MODEL)),
            _resident((1, D_MODEL)),
        ],
        out_specs=pl.BlockSpec((None, tm, D_MODEL), row),
        out_shape=jax.ShapeDtypeStruct((b, s, D_MODEL), F32),
        compiler_params=_params("parallel", "parallel"),
        name="od_out_ffn",
    )(o1, o2, o3, l1, l2, l3, u, vn, ws, bs, x3d, w_out, npost, expand, nfpre, w1, w2, nfpost)


def _rope_tables(seq):
    inv_freq = ROPE_THETA ** (-jnp.arange(0, ATT_HEAD_DIM, 2, dtype=F32) / ATT_HEAD_DIM)
    ang = jnp.arange(seq, dtype=F32)[:, None] * inv_freq[None, :]
    ang = jnp.concatenate([ang, ang, ang, ang], axis=-1)
    sign = jnp.where((jnp.arange(LANES) % 64) < 32, -1.0, 1.0).astype(F32)
    return jnp.cos(ang), jnp.sin(ang) * sign[None, :]


def _dt_lanes(per_dir):
    fwd = jnp.tile(per_dir[0], 3)
    bwd = jnp.tile(per_dir[1], 3)
    pad = jnp.zeros((64 - 3 * SSD_HEADS,), F32)
    return jnp.concatenate([fwd, pad, bwd, pad])[None, :]


def _prep_even(ev_in_proj, ev_conv_w, ev_conv_b, ssd_dt_bias, ssd_a_log, ssd_d, ssd_norm_w, ret_decay, ret_gn_w, ev_out_proj):
    wz, wxbc, wdt, wq, wk, wv, wg = jnp.split(ev_in_proj, [1024, 2560, 2592, 3104, 3616, 4640], axis=-1)
    w_main = jnp.concatenate([wz, wxbc, wq, wk, wv, wg], axis=-1).astype(BF16)
    wdt_f, wdt_b = wdt[:, :SSD_HEADS], wdt[:, SSD_HEADS:]
    zpad = jnp.zeros((D_MODEL, 64 - 3 * SSD_HEADS), F32)
    w_dt = jnp.concatenate([wdt_f, wdt_f, wdt_f, zpad, wdt_b, wdt_b, wdt_b, zpad], axis=-1).astype(BF16)
    conv_w = jnp.concatenate([ev_conv_w, jnp.zeros((8 - SSD_CONV, SSD_CONV_CH), F32)], axis=0)
    return dict(
        w_main=w_main, w_dt=w_dt, dt_bias=_dt_lanes(ssd_dt_bias), a_log=_dt_lanes(ssd_a_log),
        conv_w=conv_w, conv_b=ev_conv_b[None, :],
        dskip=jnp.repeat(ssd_d, SSD_HEAD_DIM)[None, :], nssd=ssd_norm_w[None, :], ngn=ret_gn_w[None, :],
        rd=jnp.broadcast_to(ret_decay[:, :, None], (2, RET_HEADS, LANES)),
        w_out=ev_out_proj.astype(BF16),
    )


def _even_layer(x2d, b, s, p, nmix_pre, nmix_post, ffn, cos, sin):
    z, xbc_act, q, k, v, g, dt, dtt = _ev_inproj(x2d.reshape(b, s, D_MODEL), nmix_pre, p["w_main"], p["w_dt"], p["dt_bias"],
                                                 p["a_log"], p["conv_w"], p["conv_b"], cos, sin)
    y = _ssd_ret(xbc_act, dt, dtt, q, k, v, p["rd"])
    t = b * s
    return _ev_out(y.reshape(2, t, EV_MIX), xbc_act.reshape(t, SSD_CONV_CH), z.reshape(t, SSD_WIDTH),
                   g.reshape(t, RET_V_WIDTH), x2d, p["dskip"], p["nssd"], p["ngn"], p["w_out"], nmix_post, *ffn)


def _odd_layer(x2d, b, s, p, nmix_pre, nmix_post, ffn, cos, sin):
    x3d = x2d.reshape(b, s, D_MODEL)
    qkv1, qkv4, qkv16, u, vn = _od_inproj(x3d, nmix_pre, p["w_in"], p["gnw"], cos, sin)
    (o1, l1), (o2, l2), (o3, l3) = [_attn(qkv, s) for qkv in (qkv1[:, None], qkv4, qkv16)]
    out = _od_out(o1, o2, o3, l1, l2, l3, u, vn, p["ws"], p["bs"], x3d, p["w_out"], nmix_post, *ffn)
    return out.reshape(b * s, D_MODEL)


def _trunk(x, even, odd, norm_mix_pre, norm_mix_post, norm_ffn_pre, norm_ffn_post, w1, w2):
    b, s, _ = x.shape
    cos, sin = _rope_tables(s)
    x2d = x.reshape(b * s, D_MODEL)
    depth = norm_mix_pre.shape[0]
    for i in range(depth):
        j = i // 2
        layer = _even_layer if i % 2 == 0 else _odd_layer
        params = even[j] if i % 2 == 0 else odd[j]
        ffn = (norm_ffn_pre[i][None, :], w1[i], w2[i], norm_ffn_post[i][None, :])
        x2d = layer(x2d, b, s, params, norm_mix_pre[i][None, :], norm_mix_post[i][None, :], ffn, cos, sin)
    return x2d.reshape(b, s, D_MODEL)


def kernel(x_prompt, x_sample, norm_mix_pre, norm_mix_post, norm_ffn_pre, norm_ffn_post, ffn_w1, ffn_w2, ev_in_proj, ev_conv_w, ev_conv_b, ssd_dt_bias, ssd_a_log, ssd_d, ssd_norm_w, ret_decay, ret_gn_w, ev_out_proj, od_in_proj, gmlp_norm_w, gmlp_ws, gmlp_bs, od_out_proj):
    even = [_prep_even(ev_in_proj[j], ev_conv_w[j], ev_conv_b[j], ssd_dt_bias[j], ssd_a_log[j], ssd_d[j], ssd_norm_w[j],
                       ret_decay[j], ret_gn_w[j], ev_out_proj[j]) for j in range(ev_in_proj.shape[0])]
    odd = [dict(w_in=od_in_proj[j].astype(BF16), gnw=gmlp_norm_w[j][None, :], ws=gmlp_ws[j].astype(BF16),
                bs=jnp.repeat(gmlp_bs[j].T, GMLP_GROUP_DIM, axis=1), w_out=od_out_proj[j].astype(BF16))
           for j in range(od_in_proj.shape[0])]
    w1 = ffn_w1.astype(BF16)
    w2 = ffn_w2.astype(BF16)
    run = functools.partial(_trunk, even=even, odd=odd, norm_mix_pre=norm_mix_pre, norm_mix_post=norm_mix_post,
                            norm_ffn_pre=norm_ffn_pre, norm_ffn_post=norm_ffn_post, w1=w1, w2=w2)
    return (run(x_prompt), run(x_sample))
```

```python
import functools
import math

import jax
import jax.numpy as jnp
from jax import lax
from jax.experimental import pallas as pl
from jax.experimental.pallas import tpu as pltpu

F32 = jnp.float32
BF16 = jnp.bfloat16

D_MODEL = 1024
NORM_EPS = 1e-6
ROPE_THETA = 10000.0
CHUNK = 128
LANES = 128

SSD_HEADS = 16
SSD_HEAD_DIM = 64
SSD_WIDTH = SSD_HEADS * SSD_HEAD_DIM
SSD_GROUPS = 2
SSD_HPG = SSD_HEADS // SSD_GROUPS
SSD_STATE = 128
SSD_CONV = 5
SSD_BC = 2 * SSD_GROUPS * SSD_STATE
SSD_CONV_CH = SSD_WIDTH + SSD_BC

RET_HEADS = 8
RET_QK_DIM = 64
RET_V_DIM = 128
RET_QK_WIDTH = RET_HEADS * RET_QK_DIM
RET_V_WIDTH = RET_HEADS * RET_V_DIM
EV_MIX = SSD_WIDTH + RET_V_WIDTH

ATT_HEADS = 16
ATT_HEAD_DIM = 64
ATT_WIDTH = ATT_HEADS * ATT_HEAD_DIM
DILATED_PATTERNS = ((128, 1), (512, 4), (2048, 16))
ATT_HALF = 64

GMLP_GROUPS = 8
GMLP_GROUP_DIM = 64
GMLP_WIDTH = GMLP_GROUPS * GMLP_GROUP_DIM
OD_MIX = ATT_WIDTH + GMLP_WIDTH
FFN_HIDDEN = 4 * D_MODEL

EV_Z0, EV_XBC0, EV_Q0, EV_K0, EV_V0, EV_G0, EV_END = 0, 1024, 2560, 3072, 3584, 4608, 5632
DT_LANE, CS_LANE, TOT_LANE = 0, 16, 32

ROW_TILE = 256
FFN_ROW_TILE = 512
SUB_ROWS = 256
CONV_HALO = 16
CONV_COLS = 256
VMEM_LIMIT = 56 * 1024 * 1024
NEG_BIG = -1e30
LOG2E = 1.4426950408889634
LN2 = 0.6931471805599453


def _params(*sem):
    return pltpu.CompilerParams(dimension_semantics=sem, vmem_limit_bytes=VMEM_LIMIT)


def _rms(x, w):
    return x * lax.rsqrt(jnp.mean(x * x, axis=-1, keepdims=True) + NORM_EPS) * w


def _silu(x):
    return x / (1.0 + jnp.exp(-x))


def _softplus(x):
    return jnp.maximum(x, 0.0) + jnp.log1p(jnp.exp(-jnp.abs(x)))


def _rope(t, cos, sin_signed, first_half):
    rot = jnp.where(first_half, pltpu.roll(t, 96, 1), pltpu.roll(t, 32, 1))
    return t * cos + rot * sin_signed


def _rope_cols(a, cos, sin_signed):
    lane = lax.broadcasted_iota(jnp.int32, (a.shape[0], LANES), 1)
    first_half = (lane % 64) < 32
    return [_rope(a[:, j * LANES:(j + 1) * LANES], cos, sin_signed, first_half) for j in range(a.shape[1] // LANES)]


def _split3(x):
    hi = x.astype(BF16)
    r1 = x - hi.astype(F32)
    mid = r1.astype(BF16)
    lo = (r1 - mid.astype(F32)).astype(BF16)
    return hi, mid, lo


def _ev_inproj_kernel(x_ref, xp_ref, xn_ref, nw_ref, w_ref, wdt_ref, dtb_ref, alog_ref, cw_ref, cb_ref, cos_ref, sin_ref,
                      z_ref, xbc_ref, q_ref, k_ref, v_ref, g_ref, dt_ref, dtt_ref, ext_ref):
    i = pl.program_id(1)
    last = pl.num_programs(1) - 1
    tm = x_ref.shape[0]
    nw = nw_ref[...]
    hn = _rms(x_ref[...], nw).astype(BF16)

    def mm(c0, c1):
        return jnp.dot(hn, w_ref[:, c0:c1], preferred_element_type=F32)

    cos, sin = cos_ref[...], sin_ref[...]

    def seg_z():
        z_ref[...] = mm(EV_Z0, EV_XBC0).astype(BF16)

    def seg_q():
        for j, t in enumerate(_rope_cols(mm(EV_Q0, EV_K0), cos, sin)):
            q_ref[:, j * LANES:(j + 1) * LANES] = t.astype(BF16)

    def seg_k():
        for j, t in enumerate(_rope_cols(mm(EV_K0, EV_V0), cos, sin)):
            k_ref[:, j * LANES:(j + 1) * LANES] = (t * (RET_QK_DIM ** -0.5)).astype(BF16)

    def seg_v():
        v_ref[...] = mm(EV_V0, EV_G0).astype(BF16)

    def seg_g():
        g_ref[...] = mm(EV_G0, EV_END).astype(BF16)

    hn_ext = jnp.concatenate([_rms(xp_ref[...], nw).astype(BF16), hn, _rms(xn_ref[...], nw).astype(BF16)], axis=0)
    erow = lax.broadcasted_iota(jnp.int32, (tm + 2 * CONV_HALO, 1), 0)
    lo = jnp.where(i > 0, 0, CONV_HALO)
    hi = jnp.where(i < last, tm + 2 * CONV_HALO, tm + CONV_HALO)
    inside = (erow >= lo) & (erow < hi)
    others = [seg_z, seg_q, seg_k, seg_v, seg_g]
    for blk, c0 in enumerate(range(0, SSD_CONV_CH, CONV_COLS)):
        cols = slice(c0, c0 + CONV_COLS)
        ext = jnp.dot(hn_ext, w_ref[:, EV_XBC0 + c0:EV_XBC0 + c0 + CONV_COLS], preferred_element_type=F32)
        ext_ref[blk % 2] = jnp.where(inside, ext, 0.0)
        acc = jnp.broadcast_to(cb_ref[:, cols], (tm, CONV_COLS))
        for j in range(SSD_CONV):
            off = CONV_HALO - SSD_CONV // 2 + j
            acc = acc + cw_ref[j:j + 1, cols] * ext_ref[blk % 2, off:off + tm, :]
        xbc_ref[:, cols] = _silu(acc).astype(BF16)
        if blk < len(others):
            others[blk]()

    raw = jnp.dot(hn, wdt_ref[...], preferred_element_type=F32)
    dt = _softplus(raw + dtb_ref[...])
    la = dt * (-jnp.exp(alog_ref[...]))
    lane = lax.broadcasted_iota(jnp.int32, (CHUNK, LANES), 1)
    sub = lane % 64
    is_dt = sub < CS_LANE
    is_cs = (sub >= CS_LANE) & (sub < TOT_LANE)
    is_bwd = lane >= 64
    row = lax.broadcasted_iota(jnp.int32, (CHUNK, CHUNK), 0)
    col = lax.broadcasted_iota(jnp.int32, (CHUNK, CHUNK), 1)
    tri = jnp.where(row >= col, 1.0, 0.0).astype(BF16)
    for c in range(tm // CHUNK):
        rows = slice(c * CHUNK, (c + 1) * CHUNK)
        la_c = la[rows]
        cs = sum(jnp.dot(tri, part, preferred_element_type=F32) for part in _split3(la_c))
        tot = jnp.broadcast_to(cs[CHUNK - 1:CHUNK, :], cs.shape)
        rcs = tot - cs + la_c
        rec = jnp.where(is_dt, dt[rows], jnp.where(is_cs, jnp.where(is_bwd, rcs, cs), tot))
        rec_b = pltpu.roll(rec, 64, 1)
        dt_ref[0, rows, :] = rec
        dt_ref[1, rows, :] = rec_b
        dtt_ref[0, rows, :] = rec.T
        dtt_ref[1, rows, :] = rec_b.T


def _ev_inproj(x3d, nw, w_main, w_dt, dt_bias, a_log, conv_w, conv_b, cos, sin):
    b, s, _ = x3d.shape
    tm = ROW_TILE
    per = tm // CONV_HALO
    nh = s // CONV_HALO
    row = lambda bi, i: (bi, i, 0)
    const = lambda bi, i: (0, 0)
    bf = lambda n: jax.ShapeDtypeStruct((b, s, n), BF16)
    rec = pl.BlockSpec((2, None, tm, LANES), lambda bi, i: (0, bi, i, 0))
    return pl.pallas_call(
        _ev_inproj_kernel,
        grid=(b, s // tm),
        in_specs=[
            pl.BlockSpec((None, tm, D_MODEL), row),
            pl.BlockSpec((None, CONV_HALO, D_MODEL), lambda bi, i: (bi, jnp.maximum(i * per - 1, 0), 0)),
            pl.BlockSpec((None, CONV_HALO, D_MODEL), lambda bi, i: (bi, jnp.minimum((i + 1) * per, nh - 1), 0)),
            pl.BlockSpec((1, D_MODEL), const),
            pl.BlockSpec(w_main.shape, const),
            pl.BlockSpec(w_dt.shape, const),
            pl.BlockSpec((1, LANES), const),
            pl.BlockSpec((1, LANES), const),
            pl.BlockSpec(conv_w.shape, const),
            pl.BlockSpec((1, SSD_CONV_CH), const),
            pl.BlockSpec((tm, LANES), lambda bi, i: (i, 0)),
            pl.BlockSpec((tm, LANES), lambda bi, i: (i, 0)),
        ],
        out_specs=[
            pl.BlockSpec((None, tm, SSD_WIDTH), row),
            pl.BlockSpec((None, tm, SSD_CONV_CH), row),
            pl.BlockSpec((None, tm, RET_QK_WIDTH), row),
            pl.BlockSpec((None, tm, RET_QK_WIDTH), row),
            pl.BlockSpec((None, tm, RET_V_WIDTH), row),
            pl.BlockSpec((None, tm, RET_V_WIDTH), row),
            rec, rec,
        ],
        out_shape=[bf(SSD_WIDTH), bf(SSD_CONV_CH), bf(RET_QK_WIDTH), bf(RET_QK_WIDTH), bf(RET_V_WIDTH), bf(RET_V_WIDTH),
                   jax.ShapeDtypeStruct((2, b, s, LANES), F32), jax.ShapeDtypeStruct((2, b, s, LANES), F32)],
        scratch_shapes=[pltpu.VMEM((2, tm + 2 * CONV_HALO, CONV_COLS), F32)],
        compiler_params=_params("parallel", "parallel"),
        name="ev_inproj",
    )(x3d, x3d, x3d, nw, w_main, w_dt, dt_bias, a_log, conv_w, conv_b, cos, sin)


SSD_STEP_CHUNKS = 4


def _ssd_ret_kernel(xs_ref, bc_ref, dt_ref, dtt_ref, q_ref, k_ref, v_ref, rd_ref, y_out_ref, h_ref, r_ref, dm_ref, dec_ref):
    d = pl.program_id(1)
    step = pl.program_id(2)
    fwd = d == 0
    row = lax.broadcasted_iota(jnp.int32, (CHUNK, CHUNK), 0)
    col = lax.broadcasted_iota(jnp.int32, (CHUNK, CHUNK), 1)
    ahead = jnp.where(fwd, row - col, col - row)
    causal = ahead >= 0
    first = col < SSD_HEAD_DIM
    first_row = first[0:1, :]
    m_lo = jnp.where(first_row, 1.0, 0.0).astype(BF16)
    m_hi = jnp.where(first_row, 0.0, 1.0).astype(BF16)

    @pl.when(step == 0)
    def _():
        h_ref[...] = jnp.zeros_like(h_ref)
        r_ref[...] = jnp.zeros_like(r_ref)
        rd = rd_ref[...]
        lg = jnp.minimum(rd, 0.0) - jnp.log1p(jnp.exp(-jnp.abs(rd)))
        rowf = row.astype(F32)
        dist = ahead.astype(F32)
        posq = jnp.where(fwd, rowf + 1.0, CHUNK - rowf)
        posk = jnp.where(fwd, CHUNK - 1.0 - rowf, rowf)
        for h in range(RET_HEADS):
            dm_ref[h] = jnp.where(causal, jnp.exp(lg[h:h + 1, :] * dist), 0.0)
        for pr in range(RET_HEADS // 2):
            lg_a, lg_b = lg[2 * pr:2 * pr + 1, :], lg[2 * pr + 1:2 * pr + 2, :]
            lg2 = jnp.where(first_row, lg_a, lg_b)
            dec_ref[0, pr] = jnp.exp(lg2 * posq)
            dec_ref[1, pr] = jnp.exp(lg2 * posk)
            dec_ref[2, pr] = jnp.exp(jnp.where(row < RET_QK_DIM, lg_a, lg_b) * float(CHUNK))

    nt = (((1,), (1,)), ((), ()))

    def ret_pair(pr, rows, y_ref):
        cols = slice(pr * LANES, (pr + 1) * LANES)
        q2 = q_ref[rows, cols]
        k2 = k_ref[rows, cols]
        rp = r_ref[pr]
        rpb = rp.astype(BF16)
        qd2 = (q2.astype(F32) * dec_ref[0, pr]).astype(BF16)
        kdt = (k2.astype(F32) * dec_ref[1, pr]).T.astype(BF16)
        st = []
        for half, msk in enumerate((m_lo, m_hi)):
            h = 2 * pr + half
            vh = v_ref[rows, h * RET_V_DIM:(h + 1) * RET_V_DIM]
            sc = lax.dot_general(q2 * msk, k2, nt, preferred_element_type=F32) * dm_ref[h]
            y = (jnp.dot(sc.astype(BF16), vh, preferred_element_type=F32)
                 + jnp.dot(qd2 * msk, rpb, preferred_element_type=F32))
            y_ref[rows, SSD_WIDTH + h * RET_V_DIM:SSD_WIDTH + (h + 1) * RET_V_DIM] = y.astype(BF16)
            st.append(jnp.dot(kdt[half * RET_QK_DIM:(half + 1) * RET_QK_DIM, :], vh, preferred_element_type=F32))
        r_ref[pr] = rp * dec_ref[2, pr] + jnp.concatenate(st, axis=0)

    def ssd_pair(g, pr, cb, hg, yoff, bgt, rows, rec2, rect, rect2, y_ref):
        cols = slice((g * SSD_HPG // 2 + pr) * LANES, (g * SSD_HPG // 2 + pr + 1) * LANES)
        lanes = slice(pr * LANES, (pr + 1) * LANES)
        x2 = xs_ref[rows, cols]
        yd, csb, st, dec = [], [], [], []
        for half in range(2):
            e = g * SSD_HPG + 2 * pr + half
            cs_row = rect2[CS_LANE + e:CS_LANE + e + 1, :]
            dt_row = rect[DT_LANE + e:DT_LANE + e + 1, :]
            tot_row = rect2[TOT_LANE + e:TOT_LANE + e + 1, :]
            csb.append(jnp.broadcast_to(rec2[:, CS_LANE + e:CS_LANE + e + 1], (CHUNK, CHUNK)))
            decay = jnp.where(causal, jnp.exp2(csb[half] - cs_row), 0.0)
            me = (cb * decay * dt_row).astype(BF16)
            yd.append(jnp.dot(me, x2, preferred_element_type=F32))
            bw = (bgt * (dt_row * jnp.exp2(tot_row - cs_row))).astype(BF16)
            st.append(jnp.dot(bw, x2, preferred_element_type=F32))
            dec.append(jnp.exp2(tot_row))
        y = jnp.where(first, yd[0], yd[1]) + yoff[:, lanes] * jnp.exp2(jnp.where(first, csb[0], csb[1]))
        y_ref[rows, cols] = y.astype(BF16)
        h_ref[g, :, lanes] = hg[:, lanes] * jnp.where(first_row, dec[0], dec[1]) + jnp.where(first, st[0], st[1])

    def one_chunk(rows):
        rec2 = dt_ref[rows, :] * LOG2E
        rect = dtt_ref[rows, :]
        rect2 = rect * LOG2E
        for g in range(SSD_GROUPS):
            bg = bc_ref[rows, g * SSD_STATE:(g + 1) * SSD_STATE]
            cg = bc_ref[rows, (SSD_GROUPS + g) * SSD_STATE:(SSD_GROUPS + g + 1) * SSD_STATE]
            cb = lax.dot_general(cg, bg, nt, preferred_element_type=F32)
            hg = h_ref[g]
            yoff = jnp.dot(cg, hg.astype(BF16), preferred_element_type=F32)
            bgt = bg.astype(F32).T
            for pr in range(SSD_HPG // 2):
                if pr % 2 == 0:
                    ret_pair(g * 2 + pr // 2, rows, y_out_ref)
                ssd_pair(g, pr, cb, hg, yoff, bgt, rows, rec2, rect, rect2, y_out_ref)

    for k in range(SSD_STEP_CHUNKS):
        sub = jnp.where(fwd, k, SSD_STEP_CHUNKS - 1 - k)
        one_chunk(pl.ds(pl.multiple_of(sub * CHUNK, CHUNK), CHUNK))


def _ssd_ret(xbc_act, dt, dtt, q, k, v, rd):
    b, s, _ = xbc_act.shape
    rows = SSD_STEP_CHUNKS * CHUNK
    nc = s // rows

    def chunk(d, i):
        return i + d * (nc - 1 - 2 * i)

    seq = lambda bi, d, i: (bi, chunk(d, i), 0)
    rec = lambda bi, d, i: (d, bi, chunk(d, i), 0)
    return pl.pallas_call(
        _ssd_ret_kernel,
        grid=(b, 2, nc),
        in_specs=[
            pl.BlockSpec((None, rows, SSD_WIDTH), seq),
            pl.BlockSpec((None, rows, SSD_BC), lambda bi, d, i: (bi, chunk(d, i), SSD_WIDTH // SSD_BC)),
            pl.BlockSpec((None, None, rows, LANES), rec),
            pl.BlockSpec((None, None, rows, LANES), rec),
            pl.BlockSpec((None, rows, RET_QK_WIDTH), seq),
            pl.BlockSpec((None, rows, RET_QK_WIDTH), seq),
            pl.BlockSpec((None, rows, RET_V_WIDTH), seq),
            pl.BlockSpec((None, RET_HEADS, LANES), lambda bi, d, i: (d, 0, 0)),
        ],
        out_specs=pl.BlockSpec((None, None, rows, EV_MIX), rec),
        out_shape=jax.ShapeDtypeStruct((2, b, s, EV_MIX), BF16),
        scratch_shapes=[pltpu.VMEM((SSD_GROUPS, SSD_STATE, SSD_HPG * SSD_HEAD_DIM), F32),
                        pltpu.VMEM((RET_HEADS // 2, 2 * RET_QK_DIM, RET_V_DIM), F32),
                        pltpu.VMEM((RET_HEADS, CHUNK, CHUNK), F32),
                        pltpu.VMEM((3, RET_HEADS // 2, CHUNK, CHUNK), F32)],
        compiler_params=_params("parallel", "arbitrary", "arbitrary"),
        name="ev_ssd_ret",
    )(xbc_act, xbc_act, dt, dtt, q, k, v, rd)


FFN_HIDDEN_TILE = 1024


def _ffn_chunk(hn, w1_ref, w2_ref, c):
    cols = slice(c * FFN_HIDDEN_TILE, (c + 1) * FFN_HIDDEN_TILE)
    a = jnp.maximum(jnp.dot(hn, w1_ref[:, cols], preferred_element_type=F32), 0.0)
    return jnp.dot((a * a).astype(BF16), w2_ref[cols, :], preferred_element_type=F32)


def _mix_then_ffn(mix_rows, o_ref, nfpre_ref, w1_ref, w2_ref, nfpost_ref):
    n = o_ref.shape[0]
    half = n // 2
    chunks = range(FFN_HIDDEN // FFN_HIDDEN_TILE)
    xa = mix_rows(0, half)
    hna = _rms(xa, nfpre_ref[...]).astype(BF16)
    acc = _ffn_chunk(hna, w1_ref, w2_ref, 0)
    xb = mix_rows(half, half)
    for c in chunks[1:]:
        acc = acc + _ffn_chunk(hna, w1_ref, w2_ref, c)
    o_ref[0:half, :] = xa + _rms(acc, nfpost_ref[...])
    hnb = _rms(xb, nfpre_ref[...]).astype(BF16)
    acc = _ffn_chunk(hnb, w1_ref, w2_ref, 0)
    for c in chunks[1:]:
        acc = acc + _ffn_chunk(hnb, w1_ref, w2_ref, c)
    o_ref[half:n, :] = xb + _rms(acc, nfpost_ref[...])


def _resident(shape):
    return pl.BlockSpec(shape, lambda *_: (0,) * len(shape), pipeline_mode=pl.Buffered(1))


def _ev_out_kernel(yf_ref, yb_ref, xs_ref, z_ref, g_ref, x_ref, dskip_ref, nssd_ref, ngn_ref, w_ref, npost_ref,
                   nfpre_ref, w1_ref, w2_ref, nfpost_ref, o_ref):
    gw = SSD_WIDTH // SSD_GROUPS

    def mix_rows(start, size):
        rows = pl.ds(start, size)
        yf = yf_ref[rows, :].astype(F32)
        yb = yb_ref[rows, :].astype(F32)
        y = yf[:, :SSD_WIDTH] + yb[:, :SSD_WIDTH] + xs_ref[rows, :].astype(F32) * dskip_ref[...]
        gg = y * _silu(z_ref[rows, :].astype(F32))
        parts = []
        for g in range(SSD_GROUPS):
            part = gg[:, g * gw:(g + 1) * gw]
            part = (part * lax.rsqrt(jnp.mean(part * part, axis=-1, keepdims=True) + NORM_EPS)
                    * nssd_ref[:, g * gw:(g + 1) * gw])
            parts.append(part.astype(BF16))
        r = yf[:, SSD_WIDTH:] + yb[:, SSD_WIDTH:]
        gate = _silu(g_ref[rows, :].astype(F32))
        for h in range(RET_HEADS):
            lanes = slice(h * RET_V_DIM, (h + 1) * RET_V_DIM)
            part = r[:, lanes]
            mu = jnp.mean(part, axis=-1, keepdims=True)
            cen = part - mu
            var = jnp.mean(cen * cen, axis=-1, keepdims=True)
            parts.append((cen * lax.rsqrt(var + NORM_EPS) * ngn_ref[:, lanes] * gate[:, lanes]).astype(BF16))
        mix = jnp.dot(jnp.concatenate(parts, axis=1), w_ref[...], preferred_element_type=F32)
        return x_ref[rows, :] + _rms(mix, npost_ref[...])

    _mix_then_ffn(mix_rows, o_ref, nfpre_ref, w1_ref, w2_ref, nfpost_ref)


def _ev_out(y, xbc_act2d, z, g, x2d, dskip, nssd, ngn, w_out, npost, nfpre, w1, w2, nfpost):
    t = x2d.shape[0]
    tm = FFN_ROW_TILE
    row = lambda i: (i, 0)
    vec = lambda w: _resident((1, w))
    return pl.pallas_call(
        _ev_out_kernel,
        grid=(t // tm,),
        in_specs=[
            pl.BlockSpec((None, tm, EV_MIX), lambda i: (0, i, 0)),
            pl.BlockSpec((None, tm, EV_MIX), lambda i: (1, i, 0)),
            pl.BlockSpec((tm, SSD_WIDTH), row),
            pl.BlockSpec((tm, SSD_WIDTH), row),
            pl.BlockSpec((tm, RET_V_WIDTH), row),
            pl.BlockSpec((tm, D_MODEL), row),
            vec(SSD_WIDTH), vec(SSD_WIDTH), vec(RET_V_WIDTH),
            _resident((EV_MIX, D_MODEL)),
            vec(D_MODEL), vec(D_MODEL),
            _resident((D_MODEL, FFN_HIDDEN)), _resident((FFN_HIDDEN, D_MODEL)),
            vec(D_MODEL),
        ],
        out_specs=pl.BlockSpec((tm, D_MODEL), row),
        out_shape=jax.ShapeDtypeStruct((t, D_MODEL), F32),
        compiler_params=_params("parallel"),
        name="ev_out_ffn",
    )(y, y, xbc_act2d, z, g, x2d, dskip, nssd, ngn, w_out, npost, nfpre, w1, w2, nfpost)


OD_Q0, OD_K0, OD_V0, OD_U0, OD_VG0, OD_END = 0, 1024, 2048, 3072, 3584, 4096


QKV_SLABS = 3 * ATT_WIDTH // LANES
DILATIONS = tuple(d for _, d in DILATED_PATTERNS)
assert DILATIONS[0] == 1 and DILATIONS[2] == DILATIONS[1] ** 2


def _od_inproj_kernel(x_ref, nw_ref, w_ref, gnw_ref, cos_ref, sin_ref, qkv_ref, qkv4_ref, qkv16_ref, u_ref, vn_ref,
                      slab_ref, slab4_ref):
    tm = x_ref.shape[0]
    hn = _rms(x_ref[...], nw_ref[...]).astype(BF16)

    def mm(c0, c1):
        return jnp.dot(hn, w_ref[:, c0:c1], preferred_element_type=F32)

    cos, sin = cos_ref[...], sin_ref[...]
    r4 = DILATIONS[1]
    n4 = tm // r4

    def emit(j0, cols):
        for j, t in enumerate(cols, start=j0):
            lanes = slice(j * LANES, (j + 1) * LANES)
            qkv_ref[:, lanes] = t.astype(BF16)
            slab_ref[j] = t
            for c4 in range(r4):
                rows = slab_ref[j, pl.ds(c4, n4, stride=r4), :]
                qkv4_ref[c4, :, lanes] = rows.astype(BF16)
                slab4_ref[j, c4 * n4:(c4 + 1) * n4, :] = rows
            for c4 in range(r4):
                for c2 in range(r4):
                    rows = slab4_ref[j, pl.ds(c4 * n4 + c2, n4 // r4, stride=r4), :]
                    qkv16_ref[c4 + r4 * c2, :, lanes] = rows.astype(BF16)

    per = ATT_WIDTH // LANES
    emit(0, [t * (ATT_HEAD_DIM ** -0.5 * LOG2E) for t in _rope_cols(mm(OD_Q0, OD_K0), cos, sin)])
    emit(per, _rope_cols(mm(OD_K0, OD_V0), cos, sin))
    v = mm(OD_V0, OD_U0)
    emit(2 * per, [v[:, j * LANES:(j + 1) * LANES] for j in range(per)])
    u_ref[...] = mm(OD_U0, OD_VG0).astype(BF16)
    vg = mm(OD_VG0, OD_END)
    mu = jnp.mean(vg, axis=-1, keepdims=True)
    cen = vg - mu
    var = jnp.mean(cen * cen, axis=-1, keepdims=True)
    vn_ref[...] = (cen * lax.rsqrt(var + NORM_EPS) * gnw_ref[...]).astype(BF16)


def _od_inproj(x3d, nw, w_in, gnw, cos, sin):
    b, s, _ = x3d.shape
    tm = ROW_TILE
    row = lambda bi, i: (bi, i, 0)
    const = lambda bi, i: (0, 0)
    r4, r16 = DILATIONS[1], DILATIONS[2]
    width = 3 * ATT_WIDTH
    return pl.pallas_call(
        _od_inproj_kernel,
        grid=(b, s // tm),
        in_specs=[
            pl.BlockSpec((None, tm, D_MODEL), row),
            pl.BlockSpec((1, D_MODEL), const),
            pl.BlockSpec(w_in.shape, const),
            pl.BlockSpec((1, GMLP_WIDTH), const),
            pl.BlockSpec((tm, LANES), lambda bi, i: (i, 0)),
            pl.BlockSpec((tm, LANES), lambda bi, i: (i, 0)),
        ],
        out_specs=[
            pl.BlockSpec((None, tm, width), row),
            pl.BlockSpec((None, r4, tm // r4, width), lambda bi, i: (bi, 0, i, 0)),
            pl.BlockSpec((None, r16, tm // r16, width), lambda bi, i: (bi, 0, i, 0)),
            pl.BlockSpec((None, tm, GMLP_WIDTH), row),
            pl.BlockSpec((None, tm, GMLP_WIDTH), row),
        ],
        out_shape=[jax.ShapeDtypeStruct((b, s, width), BF16),
                   jax.ShapeDtypeStruct((b, r4, s // r4, width), BF16),
                   jax.ShapeDtypeStruct((b, r16, s // r16, width), BF16),
                   jax.ShapeDtypeStruct((b, s, GMLP_WIDTH), BF16), jax.ShapeDtypeStruct((b, s, GMLP_WIDTH), BF16)],
        scratch_shapes=[pltpu.VMEM((QKV_SLABS, tm, LANES), F32), pltpu.VMEM((QKV_SLABS, tm, LANES), F32)],
        compiler_params=_params("parallel", "parallel"),
        name="od_inproj",
    )(x3d, nw, w_in, gnw, cos, sin)


ATT_Q_TILE = 128
ATT_GROUP = 2


def _attn_kernel(q_ref, kp_ref, kc_ref, kn_ref, vp_ref, vc_ref, vn_ref, o_ref, lse_ref, *, dilation):
    i = pl.program_id(1)
    last = pl.num_programs(1) - 1
    nk = ATT_Q_TILE + 2 * ATT_HALF
    qq = lax.broadcasted_iota(jnp.int32, (ATT_Q_TILE, nk), 0)
    kk = lax.broadcasted_iota(jnp.int32, (ATT_Q_TILE, nk), 1)
    k_lo = jnp.where(i > 0, 0, ATT_HALF)
    k_hi = jnp.where(i < last, nk - 1, ATT_HALF + ATT_Q_TILE - 1)
    valid = (kk >= jnp.maximum(qq, k_lo)) & (kk <= jnp.minimum(qq + 2 * ATT_HALF, k_hi))
    lane = lax.broadcasted_iota(jnp.int32, (ATT_Q_TILE, LANES), 1)
    first = lane < ATT_HEAD_DIM
    masks = (jnp.where(first[0:1, :], 1.0, 0.0).astype(BF16), jnp.where(first[0:1, :], 0.0, 1.0).astype(BF16))
    for j in range(ATT_GROUP):
        if dilation > 1:
            rows = pl.ds(pl.program_id(2) * ATT_GROUP + j, ATT_Q_TILE, stride=dilation)
        lse_all = jnp.zeros((ATT_Q_TILE, LANES), F32)
        for pr in range(ATT_HEADS // 2):
            cols = slice(pr * LANES, (pr + 1) * LANES)
            q2 = q_ref[j, :, cols]
            k2 = jnp.concatenate([kp_ref[j, :, cols], kc_ref[j, :, cols], kn_ref[j, :, cols]], axis=0)
            v2 = jnp.concatenate([vp_ref[j, :, cols], vc_ref[j, :, cols], vn_ref[j, :, cols]], axis=0)
            outs = []
            for half in range(2):
                s = lax.dot_general(q2 * masks[half], k2, (((1,), (1,)), ((), ())), preferred_element_type=F32)
                s = jnp.where(valid, s, NEG_BIG)
                m = jnp.max(s, axis=-1, keepdims=True)
                p = jnp.exp2(s - m)
                den = jnp.sum(p, axis=-1, keepdims=True)
                outs.append(jnp.dot(p.astype(BF16), v2, preferred_element_type=F32) * (1.0 / den))
                lse_all = jnp.where(lane == 2 * pr + half, m * LN2 + jnp.log(den), lse_all)
            o_pair = jnp.where(first, outs[0], outs[1])
            if dilation > 1:
                o_ref[pr, rows, :] = o_pair
            else:
                o_ref[j, pr, :, :] = o_pair
        if dilation > 1:
            lse_ref[rows, :] = lse_all
        else:
            lse_ref[j] = lse_all


def _attn(qkv, seq):
    b, r, l, _ = qkv.shape
    tq = ATT_Q_TILE
    per = tq // ATT_HALF
    nhalf = l // ATT_HALF
    pairs = ATT_HEADS // 2
    grp = ATT_GROUP
    if r > 1:
        grid = (b, l // tq, r // grp)
        blk = lambda rows: (None, grp, rows, ATT_WIDTH)
        at = lambda row_of: (lambda col: (lambda bi, i, c: (bi, c, row_of(i), col)))
        out_specs = [pl.BlockSpec((None, pairs, tq * r, LANES), lambda bi, i, c: (bi, 0, i, 0)),
                     pl.BlockSpec((None, tq * r, LANES), lambda bi, i, c: (bi, i, 0))]
    else:
        grid = (b // grp, l // tq, 1)
        blk = lambda rows: (grp, None, rows, ATT_WIDTH)
        at = lambda row_of: (lambda col: (lambda bi, i, c: (bi, 0, row_of(i), col)))
        out_specs = [pl.BlockSpec((grp, pairs, tq, LANES), lambda bi, i, c: (bi, 0, i, 0)),
                     pl.BlockSpec((grp, tq, LANES), lambda bi, i, c: (bi, i, 0))]
    main = lambda col: pl.BlockSpec(blk(tq), at(lambda i: i)(col))
    prev = lambda col: pl.BlockSpec(blk(ATT_HALF), at(lambda i: jnp.maximum(i * per - 1, 0))(col))
    nxt = lambda col: pl.BlockSpec(blk(ATT_HALF), at(lambda i: jnp.minimum((i + 1) * per, nhalf - 1))(col))
    return pl.pallas_call(
        functools.partial(_attn_kernel, dilation=r),
        grid=grid,
        in_specs=[main(0), prev(1), main(1), nxt(1), prev(2), main(2), nxt(2)],
        out_specs=out_specs,
        out_shape=[jax.ShapeDtypeStruct((b, pairs, seq, LANES), F32), jax.ShapeDtypeStruct((b, seq, LANES), F32)],
        compiler_params=_params("parallel", "parallel", "arbitrary"),
        name=f"od_attn_r{r}",
    )(qkv, qkv, qkv, qkv, qkv, qkv, qkv)


def _od_out_kernel(o1_ref, o2_ref, o3_ref, l1_ref, l2_ref, l3_ref, u_ref, vn_ref, ws_ref, bs_ref, x_ref, w_ref,
                   npost_ref, ex_ref, nfpre_ref, w1_ref, w2_ref, nfpost_ref, out_ref):
    first = lax.broadcasted_iota(jnp.int32, (CHUNK, LANES), 1) < GMLP_GROUP_DIM

    def per_head_lanes(w):
        hi = w.astype(BF16)
        lo = (w - hi.astype(F32)).astype(BF16)
        return (jnp.dot(hi, ex_ref[...], preferred_element_type=F32) + jnp.dot(lo, ex_ref[...], preferred_element_type=F32))

    def mix_rows(start, size):
        rows = pl.ds(start, size)

        def token_major(o_ref):
            return jnp.concatenate([o_ref[pr, rows, :] for pr in range(ATT_HEADS // 2)], axis=1)

        l1, l2, l3 = l1_ref[rows, :], l2_ref[rows, :], l3_ref[rows, :]
        m = jnp.maximum(jnp.maximum(l1, l2), l3)
        e1, e2, e3 = jnp.exp(l1 - m), jnp.exp(l2 - m), jnp.exp(l3 - m)
        inv = 1.0 / (e1 + e2 + e3)
        o3 = token_major(o3_ref)
        att = (o3 + per_head_lanes(e1 * inv) * (token_major(o1_ref) - o3)
               + per_head_lanes(e2 * inv) * (token_major(o2_ref) - o3))
        gate_rows = []
        for c0 in range(0, size, CHUNK):
            crows = pl.ds(start + c0, CHUNK)
            parts = []
            for pr in range(GMLP_GROUPS // 2):
                cols = slice(pr * LANES, (pr + 1) * LANES)
                vn2 = vn_ref[crows, cols]
                mixed = jnp.where(first, jnp.dot(ws_ref[2 * pr], vn2, preferred_element_type=F32),
                                  jnp.dot(ws_ref[2 * pr + 1], vn2, preferred_element_type=F32)) + bs_ref[:, cols]
                parts.append((u_ref[crows, cols].astype(F32) * mixed).astype(BF16))
            gate_rows.append(jnp.concatenate(parts, axis=1))
        cat = jnp.concatenate([att.astype(BF16), jnp.concatenate(gate_rows, axis=0)], axis=1)
        mix = jnp.dot(cat, w_ref[...], preferred_element_type=F32)
        return x_ref[rows, :] + _rms(mix, npost_ref[...])

    _mix_then_ffn(mix_rows, out_ref, nfpre_ref, w1_ref, w2_ref, nfpost_ref)


def _od_out(o1, o2, o3, l1, l2, l3, u, vn, ws, bs, x3d, w_out, npost, nfpre, w1, w2, nfpost):
    b, s, _ = x3d.shape
    tm = FFN_ROW_TILE
    row = lambda bi, i: (bi, i, 0)
    slabs = pl.BlockSpec((None, ATT_HEADS // 2, tm, LANES), lambda bi, i: (bi, 0, i, 0))
    head_of_lane = jnp.arange(ATT_WIDTH) // ATT_HEAD_DIM
    expand = (jnp.arange(LANES)[:, None] == head_of_lane[None, :]).astype(BF16)
    return pl.pallas_call(
        _od_out_kernel,
        grid=(b, s // tm),
        in_specs=[
            slabs, slabs, slabs,
            pl.BlockSpec((None, tm, LANES), row), pl.BlockSpec((None, tm, LANES), row), pl.BlockSpec((None, tm, LANES), row),
            pl.BlockSpec((None, tm, GMLP_WIDTH), row), pl.BlockSpec((None, tm, GMLP_WIDTH), row),
            _resident((GMLP_GROUPS, CHUNK, CHUNK)),
            _resident((CHUNK, GMLP_WIDTH)),
            pl.BlockSpec((None, tm, D_MODEL), row),
            _resident((OD_MIX, D_MODEL)),
            _resident((1, D_MODEL)),
            _resident((LANES, ATT_WIDTH)),
            _resident((1, D_MODEL)),
            _resident((D_MODEL, FFN_HIDDEN)), _resident((FFN_HIDDEN, D_MODEL)),
            _resident((1, D_MODEL)),
        ],
        out_specs=pl.BlockSpec((None, tm, D_MODEL), row),
        out_shape=jax.ShapeDtypeStruct((b, s, D_MODEL), F32),
        compiler_params=_params("parallel", "parallel"),
        name="od_out_ffn",
    )(o1, o2, o3, l1, l2, l3, u, vn, ws, bs, x3d, w_out, npost, expand, nfpre, w1, w2, nfpost)


def _rope_tables(seq):
    inv_freq = ROPE_THETA ** (-jnp.arange(0, ATT_HEAD_DIM, 2, dtype=F32) / ATT_HEAD_DIM)
    ang = jnp.arange(seq, dtype=F32)[:, None] * inv_freq[None, :]
    ang = jnp.concatenate([ang, ang, ang, ang], axis=-1)
    sign = jnp.where((jnp.arange(LANES) % 64) < 32, -1.0, 1.0).astype(F32)
    return jnp.cos(ang), jnp.sin(ang) * sign[None, :]


def _dt_lanes(per_dir):
    fwd = jnp.tile(per_dir[0], 3)
    bwd = jnp.tile(per_dir[1], 3)
    pad = jnp.zeros((64 - 3 * SSD_HEADS,), F32)
    return jnp.concatenate([fwd, pad, bwd, pad])[None, :]


def _prep_even(ev_in_proj, ev_conv_w, ev_conv_b, ssd_dt_bias, ssd_a_log, ssd_d, ssd_norm_w, ret_decay, ret_gn_w, ev_out_proj):
    wz, wxbc, wdt, wq, wk, wv, wg = jnp.split(ev_in_proj, [1024, 2560, 2592, 3104, 3616, 4640], axis=-1)
    w_main = jnp.concatenate([wz, wxbc, wq, wk, wv, wg], axis=-1).astype(BF16)
    wdt_f, wdt_b = wdt[:, :SSD_HEADS], wdt[:, SSD_HEADS:]
    zpad = jnp.zeros((D_MODEL, 64 - 3 * SSD_HEADS), F32)
    w_dt = jnp.concatenate([wdt_f, wdt_f, wdt_f, zpad, wdt_b, wdt_b, wdt_b, zpad], axis=-1).astype(BF16)
    conv_w = jnp.concatenate([ev_conv_w, jnp.zeros((8 - SSD_CONV, SSD_CONV_CH), F32)], axis=0)
    return dict(
        w_main=w_main, w_dt=w_dt, dt_bias=_dt_lanes(ssd_dt_bias), a_log=_dt_lanes(ssd_a_log),
        conv_w=conv_w, conv_b=ev_conv_b[None, :],
        dskip=jnp.repeat(ssd_d, SSD_HEAD_DIM)[None, :], nssd=ssd_norm_w[None, :], ngn=ret_gn_w[None, :],
        rd=jnp.broadcast_to(ret_decay[:, :, None], (2, RET_HEADS, LANES)),
        w_out=ev_out_proj.astype(BF16),
    )


def _even_layer(x2d, b, s, p, nmix_pre, nmix_post, ffn, cos, sin):
    z, xbc_act, q, k, v, g, dt, dtt = _ev_inproj(x2d.reshape(b, s, D_MODEL), nmix_pre, p["w_main"], p["w_dt"], p["dt_bias"],
                                                 p["a_log"], p["conv_w"], p["conv_b"], cos, sin)
    y = _ssd_ret(xbc_act, dt, dtt, q, k, v, p["rd"])
    t = b * s
    return _ev_out(y.reshape(2, t, EV_MIX), xbc_act.reshape(t, SSD_CONV_CH), z.reshape(t, SSD_WIDTH),
                   g.reshape(t, RET_V_WIDTH), x2d, p["dskip"], p["nssd"], p["ngn"], p["w_out"], nmix_post, *ffn)


def _odd_layer(x2d, b, s, p, nmix_pre, nmix_post, ffn, cos, sin):
    x3d = x2d.reshape(b, s, D_MODEL)
    qkv1, qkv4, qkv16, u, vn = _od_inproj(x3d, nmix_pre, p["w_in"], p["gnw"], cos, sin)
    (o1, l1), (o2, l2), (o3, l3) = [_attn(qkv, s) for qkv in (qkv1[:, None], qkv4, qkv16)]
    out = _od_out(o1, o2, o3, l1, l2, l3, u, vn, p["ws"], p["bs"], x3d, p["w_out"], nmix_post, *ffn)
    return out.reshape(b * s, D_MODEL)


def _trunk(x, even, odd, norm_mix_pre, norm_mix_post, norm_ffn_pre, norm_ffn_post, w1, w2):
    b, s, _ = x.shape
    cos, sin = _rope_tables(s)
    x2d = x.reshape(b * s, D_MODEL)
    depth = norm_mix_pre.shape[0]
    for i in range(depth):
        j = i // 2
        layer = _even_layer if i % 2 == 0 else _odd_layer
        params = even[j] if i % 2 == 0 else odd[j]
        ffn = (norm_ffn_pre[i][None, :], w1[i], w2[i], norm_ffn_post[i][None, :])
        x2d = layer(x2d, b, s, params, norm_mix_pre[i][None, :], norm_mix_post[i][None, :], ffn, cos, sin)
    return x2d.reshape(b, s, D_MODEL)


def kernel(x_prompt, x_sample, norm_mix_pre, norm_mix_post, norm_ffn_pre, norm_ffn_post, ffn_w1, ffn_w2, ev_in_proj, ev_conv_w, ev_conv_b, ssd_dt_bias, ssd_a_log, ssd_d, ssd_norm_w, ret_decay, ret_gn_w, ev_out_proj, od_in_proj, gmlp_norm_w, gmlp_ws, gmlp_bs, od_out_proj):
    even = [_prep_even(ev_in_proj[j], ev_conv_w[j], ev_conv_b[j], ssd_dt_bias[j], ssd_a_log[j], ssd_d[j], ssd_norm_w[j],
                       ret_decay[j], ret_gn_w[j], ev_out_proj[j]) for j in range(ev_in_proj.shape[0])]
    odd = [dict(w_in=od_in_proj[j].astype(BF16), gnw=gmlp_norm_w[j][None, :], ws=gmlp_ws[j].astype(BF16),
                bs=jnp.repeat(gmlp_bs[j].T, GMLP_GROUP_DIM, axis=1), w_out=od_out_proj[j].astype(BF16))
           for j in range(od_in_proj.shape[0])]
    w1 = ffn_w1.astype(BF16)
    w2 = ffn_w2.astype(BF16)
    run = functools.partial(_trunk, even=even, odd=odd, norm_mix_pre=norm_mix_pre, norm_mix_post=norm_mix_post,
                            norm_ffn_pre=norm_ffn_pre, norm_ffn_post=norm_ffn_post, w1=w1, w2=w2)
    return (run(x_prompt), run(x_sample))
```
